```python
import math
import jax, jax.numpy as jnp
from jax import lax
import numpy as np

D_MODEL = 1024
BATCH = 8
SEQ = 4096
DEPTH = 4

N_MIXERS = 2
SSM_GROUP = 16
SSM_GROUPS = D_MODEL // SSM_GROUP
SSM_STATE = 64
CONV_WIDTH = 3
FFN_HIDDEN = ((8 * D_MODEL + 3 * 256 - 1) // (3 * 256)) * 256
N_SSM_LAYERS = (DEPTH + 1) // 2
N_CONV_LAYERS = DEPTH // 2
RMS_EPS = 1e-6
DT_MIN = 1e-3
DT_MAX = 1e-1

kernel_name = "hybrid_s5_shortconv_adaln"


def rms_norm(x, g):
    x32 = x.astype(jnp.float32)
    y = x32 * lax.rsqrt(jnp.mean(x32 * x32, axis=-1, keepdims=True) + RMS_EPS)
    return (y * g.astype(jnp.float32)).astype(x.dtype)


def modulate(h, shift, scale):
    return h * (1.0 + scale) + shift


def _ssm_combine(e1, e2):
    a1r, a1i, b1r, b1i = e1
    a2r, a2i, b2r, b2i = e2
    ar = a2r * a1r - a2i * a1i
    ai = a2r * a1i + a2i * a1r
    br = a2r * b1r - a2i * b1i + b2r
    bi = a2r * b1i + a2i * b1r + b2i
    return (ar, ai, br, bi)


def s5_mixer(h, a_re, a_im, log_step, b_re, b_im, c_re, c_im, d, w_out):
    bsz, seqlen, _ = h.shape
    f32 = jnp.float32
    u = h.astype(f32).reshape(bsz, seqlen, SSM_GROUPS, SSM_GROUP)
    lr = jnp.minimum(a_re.astype(f32), -1e-4)
    li = a_im.astype(f32)
    dt = jnp.exp(log_step.astype(f32))[:, None]
    mag = jnp.exp(lr * dt)
    abr = mag * jnp.cos(li * dt)
    abi = mag * jnp.sin(li * dt)
    den = lr * lr + li * li
    qr = ((abr - 1.0) * lr + abi * li) / den
    qi = (abi * lr - (abr - 1.0) * li) / den
    br, bim = b_re.astype(f32), b_im.astype(f32)
    bbar_re = qr[..., None] * br - qi[..., None] * bim
    bbar_im = qr[..., None] * bim + qi[..., None] * br
    bu_re = jnp.einsum('blgh,gph->blgp', u, bbar_re)
    bu_im = jnp.einsum('blgh,gph->blgp', u, bbar_im)
    a_r = jnp.broadcast_to(abr[None, None], (1, seqlen, SSM_GROUPS, SSM_STATE))
    a_i = jnp.broadcast_to(abi[None, None], (1, seqlen, SSM_GROUPS, SSM_STATE))
    _, _, xr, xi = lax.associative_scan(_ssm_combine, (a_r, a_i, bu_re, bu_im), axis=1)
    y = (jnp.einsum('ghp,blgp->blgh', c_re.astype(f32), xr)
         - jnp.einsum('ghp,blgp->blgh', c_im.astype(f32), xi))
    y = y.reshape(bsz, seqlen, D_MODEL) + d.astype(f32) * h.astype(f32)
    y = jax.nn.gelu(y).astype(h.dtype)
    val, gate = jnp.split(y @ w_out, 2, axis=-1)
    return val * jax.nn.sigmoid(gate)


def short_conv_mixer(h, w_in, conv_w, w_out):
    bg, cg, v = jnp.split(h @ w_in, 3, axis=-1)
    cv = cg * v
    conv = lax.conv_general_dilated(
        cv, conv_w[:, None, :].astype(cv.dtype), window_strides=(1,),
        padding=[(CONV_WIDTH - 1, 0)], dimension_numbers=('NWC', 'WIO', 'NWC'),
        feature_group_count=D_MODEL)
    return (bg * conv) @ w_out


def swiglu(h, w_in, w_out):
    g, u = jnp.split(h @ w_in, 2, axis=-1)
    return (jax.nn.silu(g) * u) @ w_out


def setup_inputs(seed: int = 0) -> dict:
    key = jax.random.key(seed)
    ks = jax.random.split(key, 24)
    D, G, P, H, F = D_MODEL, SSM_GROUPS, SSM_STATE, SSM_GROUP, FFN_HIDDEN
    nrm = jax.random.normal
    x = nrm(ks[0], (BATCH, SEQ, D), jnp.float32)
    c = nrm(ks[1], (BATCH, D), jnp.float32)
    norm1_g = 1.0 + 0.02 * nrm(ks[2], (DEPTH, D), jnp.float32)
    norm2_g = 1.0 + 0.02 * nrm(ks[3], (DEPTH, D), jnp.float32)
    w_ada = 0.5 * D ** -0.5 * nrm(ks[4], (DEPTH, D, 6 * D), jnp.float32)
    b_ada = 0.01 * nrm(ks[5], (DEPTH, 6 * D), jnp.float32)
    n_idx = jnp.arange(P, dtype=jnp.float32)
    ssm_a_re = -0.5 + 0.01 * nrm(ks[6], (N_SSM_LAYERS, G, P), jnp.float32)
    ssm_a_im = math.pi * n_idx + 0.01 * nrm(ks[7], (N_SSM_LAYERS, G, P), jnp.float32)
    ssm_log_step = jax.random.uniform(ks[8], (N_SSM_LAYERS, G), jnp.float32,
                                      math.log(DT_MIN), math.log(DT_MAX))
    ssm_b_re = (2 * H) ** -0.5 * nrm(ks[9], (N_SSM_LAYERS, G, P, H), jnp.float32)
    ssm_b_im = (2 * H) ** -0.5 * nrm(ks[10], (N_SSM_LAYERS, G, P, H), jnp.float32)
    ssm_c_re = (2 * P) ** -0.5 * nrm(ks[11], (N_SSM_LAYERS, G, H, P), jnp.float32)
    ssm_c_im = (2 * P) ** -0.5 * nrm(ks[12], (N_SSM_LAYERS, G, H, P), jnp.float32)
    ssm_d = nrm(ks[13], (N_SSM_LAYERS, D), jnp.float32)
    ssm_w_out = D ** -0.5 * nrm(ks[14], (N_SSM_LAYERS, D, 2 * D), jnp.float32)
    conv_w_in = D ** -0.5 * nrm(ks[15], (N_CONV_LAYERS, D, 3 * D), jnp.float32)
    conv_w = CONV_WIDTH ** -0.5 * nrm(ks[16], (N_CONV_LAYERS, CONV_WIDTH, D), jnp.float32)
    conv_w_out = D ** -0.5 * nrm(ks[17], (N_CONV_LAYERS, D, D), jnp.float32)
    w_ffn_in = D ** -0.5 * nrm(ks[18], (DEPTH, D, 2 * F), jnp.float32)
    w_ffn_out = F ** -0.5 * nrm(ks[19], (DEPTH, F, D), jnp.float32)
    final_g = 1.0 + 0.02 * nrm(ks[20], (D,), jnp.float32)
    return {"x": x, "c": c, "norm1_g": norm1_g, "norm2_g": norm2_g,
            "w_ada": w_ada, "b_ada": b_ada,
            "ssm_a_re": ssm_a_re, "ssm_a_im": ssm_a_im, "ssm_log_step": ssm_log_step,
            "ssm_b_re": ssm_b_re, "ssm_b_im": ssm_b_im,
            "ssm_c_re": ssm_c_re, "ssm_c_im": ssm_c_im,
            "ssm_d": ssm_d, "ssm_w_out": ssm_w_out,
            "conv_w_in": conv_w_in, "conv_w": conv_w, "conv_w_out": conv_w_out,
            "w_ffn_in": w_ffn_in, "w_ffn_out": w_ffn_out, "final_g": final_g}


def reference(x, c, norm1_g, norm2_g, w_ada, b_ada,
              ssm_a_re, ssm_a_im, ssm_log_step, ssm_b_re, ssm_b_im,
              ssm_c_re, ssm_c_im, ssm_d, ssm_w_out,
              conv_w_in, conv_w, conv_w_out,
              w_ffn_in, w_ffn_out, final_g):
    c_act = jax.nn.silu(c)
    for i in range(DEPTH):
        mods = c_act @ w_ada[i] + b_ada[i]
        sh1, sc1, g1, sh2, sc2, g2 = [m[:, None, :] for m in jnp.split(mods, 6, axis=-1)]
        h = modulate(rms_norm(x, norm1_g[i]), sh1, sc1)
        j = i // N_MIXERS
        if i % N_MIXERS == 0:
            mix = s5_mixer(h, ssm_a_re[j], ssm_a_im[j], ssm_log_step[j],
                           ssm_b_re[j], ssm_b_im[j], ssm_c_re[j], ssm_c_im[j],
                           ssm_d[j], ssm_w_out[j])
        else:
            mix = short_conv_mixer(h, conv_w_in[j], conv_w[j], conv_w_out[j])
        x = x + g1 * mix
        h = modulate(rms_norm(x, norm2_g[i]), sh2, sc2)
        x = x + g2 * swiglu(h, w_ffn_in[i], w_ffn_out[i])
    return rms_norm(x, final_g)
```

```python
import functools
import math

import jax
import jax.numpy as jnp
from jax import lax
from jax.experimental import pallas as pl
from jax.experimental.pallas import tpu as pltpu

D_MODEL = 1024
BATCH = 8
DEPTH = 4
N_MIXERS = 2
SSM_GROUP = 16
SSM_GROUPS = D_MODEL // SSM_GROUP
SSM_STATE = 64
CONV_WIDTH = 3
RMS_EPS = 1e-6

LANES = 128
SUBLANES = 8
MXU_DIM = 256
VMEM_LIMIT_BYTES = 56 * 1024 * 1024

N_CHUNKS = SSM_GROUPS * SSM_STATE // LANES
GROUPS_PER_CHUNK = LANES // SSM_STATE
N_SLABS = D_MODEL // MXU_DIM
GROUPS_PER_SLAB = MXU_DIM // SSM_GROUP
CHUNKS_PER_SLAB = N_CHUNKS // N_SLABS
SCAN_INTERLEAVE = 4

ROWS_FFN = 512
ROWS_MIX = 512
ADA_COLS = 1536


def _const_spec(shape):
    zeros = (0,) * len(shape)
    return pl.BlockSpec(shape, lambda *_: zeros, pipeline_mode=pl.Buffered(1))


def _norm_modulate(x, gain, shift, scale):
    rows = x.shape[0]
    y = x * lax.rsqrt(jnp.mean(x * x, axis=-1, keepdims=True) + RMS_EPS)
    y = (y * gain).reshape(rows // BATCH, BATCH, D_MODEL)
    return (y * (1.0 + scale)[None] + shift[None]).reshape(rows, D_MODEL)


def _gated_residual(x, gate, update):
    rows = x.shape[0]
    upd = update.reshape(rows // BATCH, BATCH, D_MODEL) * gate[None]
    return x + upd.reshape(rows, D_MODEL)


def _ada_kernel(c_ref, w_ref, b_ref, o_ref):
    c_act = jax.nn.silu(c_ref[...])
    o_ref[0] = jnp.dot(c_act, w_ref[0], preferred_element_type=jnp.float32) + b_ref[0]


def _ada_mods(c, w_ada, b_ada):
    n_cols = 6 * D_MODEL
    return pl.pallas_call(
        _ada_kernel,
        grid=(DEPTH, n_cols // ADA_COLS),
        in_specs=[
            pl.BlockSpec((BATCH, D_MODEL), lambda i, j: (0, 0)),
            pl.BlockSpec((1, D_MODEL, ADA_COLS), lambda i, j: (i, 0, j)),
            pl.BlockSpec((1, 1, ADA_COLS), lambda i, j: (i, 0, j)),
        ],
        out_specs=pl.BlockSpec((1, BATCH, ADA_COLS), lambda i, j: (i, 0, j)),
        out_shape=jax.ShapeDtypeStruct((DEPTH, BATCH, n_cols), jnp.float32),
        compiler_params=pltpu.CompilerParams(
            dimension_semantics=("arbitrary", "arbitrary"),
            vmem_limit_bytes=VMEM_LIMIT_BYTES),
        name="ada_mods",
    )(c, w_ada, b_ada.reshape(DEPTH, 1, n_cols))


def _ssm_param_kernel(are_ref, aim_ref, ls_ref, bre_ref, bim_ref,
                      abr_ref, abi_ref, bbr_ref, bbi_ref):
    lr = jnp.minimum(are_ref[...], -1e-4)
    li = aim_ref[...]
    dt = jnp.exp(ls_ref[...])
    mag = jnp.exp(lr * dt)
    abr = mag * jnp.cos(li * dt)
    abi = mag * jnp.sin(li * dt)
    den = lr * lr + li * li
    qr = ((abr - 1.0) * lr + abi * li) / den
    qi = (abi * lr - (abr - 1.0) * li) / den
    abr_ref[...] = abr
    abi_ref[...] = abi
    br = bre_ref[...]
    bi = bim_ref[...]
    bbr_ref[...] = qr[:, None, :] * br - qi[:, None, :] * bi
    bbi_ref[...] = qr[:, None, :] * bi + qi[:, None, :] * br


def _ssm_params(a_re, a_im, log_step, b_re_ghp, b_im_ghp):
    g, p, h = SSM_GROUPS, SSM_STATE, SSM_GROUP
    f32 = jnp.float32
    return pl.pallas_call(
        _ssm_param_kernel,
        out_shape=(jax.ShapeDtypeStruct((g, p), f32), jax.ShapeDtypeStruct((g, p), f32),
                   jax.ShapeDtypeStruct((g, h, p), f32), jax.ShapeDtypeStruct((g, h, p), f32)),
        name="ssm_params",
    )(a_re, a_im, log_step.reshape(g, 1), b_re_ghp, b_im_ghp)


def _ssm_weights(abr, abi, bbr, bbi, c_re, c_im):
    f32 = jnp.float32
    jl = jnp.arange(N_CHUNKS) % CHUNKS_PER_SLAB
    l2 = jnp.arange(GROUPS_PER_CHUNK)
    gi = jnp.arange(GROUPS_PER_SLAB)
    sel = (gi[None, None, :] == (jl[:, None, None] * GROUPS_PER_CHUNK + l2[None, :, None])).astype(f32)
    bb = jnp.stack([bbr, bbi]).reshape(2, N_CHUNKS, GROUPS_PER_CHUNK, SSM_GROUP, SSM_STATE)
    wb = jnp.einsum('cjlhp,jlg->jghclp', bb, sel)
    wb = wb.reshape(N_CHUNKS, MXU_DIM, 2 * LANES).astype(jnp.bfloat16)
    cc = jnp.stack([c_re, -c_im]).reshape(2, N_SLABS, CHUNKS_PER_SLAB, GROUPS_PER_CHUNK,
                                          SSM_GROUP, SSM_STATE)
    sel_c = sel[:CHUNKS_PER_SLAB]
    wc = jnp.einsum('csklhp,klg->skclpgh', cc, sel_c)
    wc = wc.reshape(N_SLABS, CHUNKS_PER_SLAB * 2 * LANES, MXU_DIM).astype(jnp.bfloat16)
    a = jnp.stack([abr.reshape(N_CHUNKS, LANES), abi.reshape(N_CHUNKS, LANES)], axis=1)
    a = jnp.broadcast_to(a[:, :, None, :], (N_CHUNKS, 2, SUBLANES, LANES))
    return wb, wc, a


def _s5_kernel(x_ref, mods_ref, gain_ref, wb_ref, wc_ref, a_ref, d_ref, wout_ref,
               o_ref, s_ref, state_ref, *, steps):
    @pl.when(pl.program_id(0) == 0)
    def _():
        state_ref[...] = jnp.zeros_like(state_ref)

    x = x_ref[...]
    mods = mods_ref[0]
    shift, scale, gate = (mods[:, k * D_MODEL:(k + 1) * D_MODEL] for k in range(3))
    h = _norm_modulate(x, gain_ref[0], shift, scale)
    hb = h.astype(jnp.bfloat16)

    for j in range(N_CHUNKS):
        slab = j // CHUNKS_PER_SLAB
        s_ref[j] = jnp.dot(hb[:, slab * MXU_DIM:(slab + 1) * MXU_DIM], wb_ref[j],
                           preferred_element_type=jnp.float32)

    for j0 in range(0, N_CHUNKS, SCAN_INTERLEAVE):
        chunks = range(j0, j0 + SCAN_INTERLEAVE)
        a_re = [a_ref[j, 0] for j in chunks]
        a_im = [a_ref[j, 1] for j in chunks]
        init = tuple((state_ref[j, 0], state_ref[j, 1]) for j in chunks)

        def step(t, carry):
            row = pl.multiple_of(t * SUBLANES, SUBLANES)
            new = []
            for q, j in enumerate(chunks):
                xr, xi = carry[q]
                bu_r = s_ref[j, pl.ds(row, SUBLANES), pl.ds(0, LANES)]
                bu_i = s_ref[j, pl.ds(row, SUBLANES), pl.ds(LANES, LANES)]
                nr = a_re[q] * xr - a_im[q] * xi + bu_r
                ni = a_re[q] * xi + a_im[q] * xr + bu_i
                s_ref[j, pl.ds(row, SUBLANES), pl.ds(0, LANES)] = nr
                s_ref[j, pl.ds(row, SUBLANES), pl.ds(LANES, LANES)] = ni
                new.append((nr, ni))
            return tuple(new)

        final = lax.fori_loop(0, steps, step, init, unroll=2)
        for q, j in enumerate(chunks):
            state_ref[j, 0] = final[q][0]
            state_ref[j, 1] = final[q][1]

    ys = []
    for slab in range(N_SLABS):
        acc = None
        for k in range(CHUNKS_PER_SLAB):
            part = jnp.dot(s_ref[slab * CHUNKS_PER_SLAB + k].astype(jnp.bfloat16),
                           wc_ref[slab, k * 2 * LANES:(k + 1) * 2 * LANES, :],
                           preferred_element_type=jnp.float32)
            acc = part if acc is None else acc + part
        ys.append(acc)
    y = jnp.concatenate(ys, axis=-1) + d_ref[...] * h
    y = jax.nn.gelu(y).astype(jnp.bfloat16)
    z = jnp.dot(y, wout_ref[...], preferred_element_type=jnp.float32)
    mix = z[:, :D_MODEL] * jax.nn.sigmoid(z[:, D_MODEL:])
    o_ref[...] = _gated_residual(x, gate, mix)


def _s5_layer(x2, mods, layer, gain, wb, wc, a, d, w_out):
    rows = x2.shape[0]
    steps = ROWS_MIX // BATCH
    return pl.pallas_call(
        functools.partial(_s5_kernel, steps=steps),
        grid=(rows // ROWS_MIX,),
        in_specs=[
            pl.BlockSpec((ROWS_MIX, D_MODEL), lambda i: (i, 0)),
            pl.BlockSpec((1, BATCH, 6 * D_MODEL), lambda i: (layer, 0, 0)),
            pl.BlockSpec((1, 1, D_MODEL), lambda i: (layer, 0, 0)),
            _const_spec(wb.shape),
            _const_spec(wc.shape),
            _const_spec(a.shape),
            _const_spec((1, D_MODEL)),
            _const_spec(w_out.shape),
        ],
        out_specs=pl.BlockSpec((ROWS_MIX, D_MODEL), lambda i: (i, 0)),
        out_shape=jax.ShapeDtypeStruct(x2.shape, x2.dtype),
        scratch_shapes=[
            pltpu.VMEM((N_CHUNKS, ROWS_MIX, 2 * LANES), jnp.float32),
            pltpu.VMEM((N_CHUNKS, 2, SUBLANES, LANES), jnp.float32),
        ],
        compiler_params=pltpu.CompilerParams(
            dimension_semantics=("arbitrary",), vmem_limit_bytes=VMEM_LIMIT_BYTES),
        name="s5_layer",
    )(x2, mods, gain, wb, wc, a, d.reshape(1, D_MODEL), w_out)


def _conv_kernel(x_ref, mods_ref, gain_ref, win_ref, cw_ref, wout_ref, o_ref, cv_ref):
    rows = x_ref.shape[0]
    hist = (CONV_WIDTH - 1) * BATCH

    @pl.when(pl.program_id(0) == 0)
    def _():
        cv_ref[pl.ds(0, hist), :] = jnp.zeros((hist, D_MODEL), jnp.float32)

    x = x_ref[...]
    mods = mods_ref[0]
    shift, scale, gate = (mods[:, k * D_MODEL:(k + 1) * D_MODEL] for k in range(3))
    h = _norm_modulate(x, gain_ref[0], shift, scale)
    proj = jnp.dot(h.astype(jnp.bfloat16), win_ref[...], preferred_element_type=jnp.float32)
    bg = proj[:, :D_MODEL]
    cv = proj[:, D_MODEL:2 * D_MODEL] * proj[:, 2 * D_MODEL:]
    cv_ref[pl.ds(hist, rows), :] = cv
    cw = cw_ref[...]
    conv = (cw[2:3] * cv + cw[1:2] * cv_ref[pl.ds(BATCH, rows), :]
            + cw[0:1] * cv_ref[pl.ds(0, rows), :])
    cv_ref[pl.ds(0, hist), :] = cv_ref[pl.ds(rows, hist), :]
    out = jnp.dot((bg * conv).astype(jnp.bfloat16), wout_ref[...],
                  preferred_element_type=jnp.float32)
    o_ref[...] = _gated_residual(x, gate, out)


def _conv_layer(x2, mods, layer, gain, w_in, conv_w, w_out):
    rows = x2.shape[0]
    hist = (CONV_WIDTH - 1) * BATCH
    return pl.pallas_call(
        _conv_kernel,
        grid=(rows // ROWS_MIX,),
        in_specs=[
            pl.BlockSpec((ROWS_MIX, D_MODEL), lambda i: (i, 0)),
            pl.BlockSpec((1, BATCH, 6 * D_MODEL), lambda i: (layer, 0, 0)),
            pl.BlockSpec((1, 1, D_MODEL), lambda i: (layer, 0, 0)),
            _const_spec(w_in.shape),
            _const_spec(conv_w.shape),
            _const_spec(w_out.shape),
        ],
        out_specs=pl.BlockSpec((ROWS_MIX, D_MODEL), lambda i: (i, 0)),
        out_shape=jax.ShapeDtypeStruct(x2.shape, x2.dtype),
        scratch_shapes=[pltpu.VMEM((ROWS_MIX + hist, D_MODEL), jnp.float32)],
        compiler_params=pltpu.CompilerParams(
            dimension_semantics=("arbitrary",), vmem_limit_bytes=VMEM_LIMIT_BYTES),
        name="conv_layer",
    )(x2, mods, gain, w_in, conv_w, w_out)


def _ffn_kernel(x_ref, mods_ref, gain_ref, win_ref, wout_ref, fg_ref, o_ref, *, final_norm):
    hidden = wout_ref.shape[0]
    x = x_ref[...]
    mods = mods_ref[0]
    shift, scale, gate = (mods[:, k * D_MODEL:(k + 1) * D_MODEL] for k in range(3, 6))
    h = _norm_modulate(x, gain_ref[0], shift, scale)
    gu = jnp.dot(h.astype(jnp.bfloat16), win_ref[...], preferred_element_type=jnp.float32)
    act = (jax.nn.silu(gu[:, :hidden]) * gu[:, hidden:]).astype(jnp.bfloat16)
    out = jnp.dot(act, wout_ref[...], preferred_element_type=jnp.float32)
    xn = _gated_residual(x, gate, out)
    if final_norm:
        xn = xn * lax.rsqrt(jnp.mean(xn * xn, axis=-1, keepdims=True) + RMS_EPS) * fg_ref[...]
    o_ref[...] = xn


def _ffn_layer(x2, mods, layer, gain, w_in, w_out, final_g, final_norm):
    rows = x2.shape[0]
    return pl.pallas_call(
        functools.partial(_ffn_kernel, final_norm=final_norm),
        grid=(rows // ROWS_FFN,),
        in_specs=[
            pl.BlockSpec((ROWS_FFN, D_MODEL), lambda i: (i, 0)),
            pl.BlockSpec((1, BATCH, 6 * D_MODEL), lambda i: (layer, 0, 0)),
            pl.BlockSpec((1, 1, D_MODEL), lambda i: (layer, 0, 0)),
            _const_spec(w_in.shape),
            _const_spec(w_out.shape),
            _const_spec((1, D_MODEL)),
        ],
        out_specs=pl.BlockSpec((ROWS_FFN, D_MODEL), lambda i: (i, 0)),
        out_shape=jax.ShapeDtypeStruct(x2.shape, x2.dtype),
        compiler_params=pltpu.CompilerParams(
            dimension_semantics=("arbitrary",), vmem_limit_bytes=VMEM_LIMIT_BYTES),
        name="ffn_layer",
    )(x2, mods, gain, w_in, w_out, final_g.reshape(1, D_MODEL))


def kernel(x, c, norm1_g, norm2_g, w_ada, b_ada, ssm_a_re, ssm_a_im, ssm_log_step, ssm_b_re, ssm_b_im, ssm_c_re, ssm_c_im, ssm_d, ssm_w_out, conv_w_in, conv_w, conv_w_out, w_ffn_in, w_ffn_out, final_g):
    bsz, seqlen, _ = x.shape
    bf16 = jnp.bfloat16
    mods = _ada_mods(c, w_ada, b_ada)
    gain1 = norm1_g.reshape(DEPTH, 1, D_MODEL)
    gain2 = norm2_g.reshape(DEPTH, 1, D_MODEL)
    x2 = jnp.transpose(x, (1, 0, 2)).reshape(seqlen * bsz, D_MODEL)
    for i in range(DEPTH):
        j = i // N_MIXERS
        if i % N_MIXERS == 0:
            abr, abi, bbr, bbi = _ssm_params(
                ssm_a_re[j], ssm_a_im[j], ssm_log_step[j],
                jnp.swapaxes(ssm_b_re[j], 1, 2), jnp.swapaxes(ssm_b_im[j], 1, 2))
            wb, wc, a = _ssm_weights(abr, abi, bbr, bbi, ssm_c_re[j], ssm_c_im[j])
            x2 = _s5_layer(x2, mods, i, gain1, wb, wc, a, ssm_d[j], ssm_w_out[j].astype(bf16))
        else:
            x2 = _conv_layer(x2, mods, i, gain1, conv_w_in[j].astype(bf16), conv_w[j],
                             conv_w_out[j].astype(bf16))
        x2 = _ffn_layer(x2, mods, i, gain2, w_ffn_in[i].astype(bf16), w_ffn_out[i].astype(bf16),
                        final_g, final_norm=(i == DEPTH - 1))
    return jnp.transpose(x2.reshape(seqlen, bsz, D_MODEL), (1, 0, 2))
```

```python
import functools

import jax
import jax.numpy as jnp
from jax import lax
from jax.experimental import pallas as pl
from jax.experimental.pallas import tpu as pltpu

D_MODEL = 1024
BATCH = 8
DEPTH = 4
N_MIXERS = 2
SSM_GROUP = 16
SSM_GROUPS = D_MODEL // SSM_GROUP
SSM_STATE = 64
CONV_WIDTH = 3
RMS_EPS = 1e-6

LANES = 128
SUBLANES = 8
BF16_ROWS = 16
MXU_DIM = 256
VMEM_LIMIT_BYTES = 56 * 1024 * 1024

N_CHUNKS = SSM_GROUPS * SSM_STATE // LANES
GROUPS_PER_CHUNK = LANES // SSM_STATE
N_SLABS = D_MODEL // MXU_DIM
GROUPS_PER_SLAB = MXU_DIM // SSM_GROUP
CHUNKS_PER_SLAB = N_CHUNKS // N_SLABS
SLAB_STATE = CHUNKS_PER_SLAB * 2 * LANES

ROWS_FFN = 512
ROWS_MIX = 512
ADA_COLS = 1536


def _const_spec(shape, index=None):
    index = (0,) * len(shape) if index is None else index
    return pl.BlockSpec(shape, lambda *_: index, pipeline_mode=pl.Buffered(1))


def _layer_spec(shape, layer):
    return _const_spec((1,) + tuple(shape[1:]), (layer,) + (0,) * (len(shape) - 1))


def _norm_modulate(x, gain, shift, scale):
    rows = x.shape[0]
    y = x * lax.rsqrt(jnp.mean(x * x, axis=-1, keepdims=True) + RMS_EPS)
    y = (y * gain).reshape(rows // BATCH, BATCH, D_MODEL)
    return (y * (1.0 + scale)[None] + shift[None]).reshape(rows, D_MODEL)


def _gated_residual(x, gate, update):
    rows = x.shape[0]
    upd = update.reshape(rows // BATCH, BATCH, D_MODEL) * gate[None]
    return x + upd.reshape(rows, D_MODEL)


def _ada_kernel(c_ref, w_ref, b_ref, o_ref):
    c_act = jax.nn.silu(c_ref[...])
    o_ref[0] = jnp.dot(c_act, w_ref[0], preferred_element_type=jnp.float32) + b_ref[0]


def _ada_mods(c, w_ada, b_ada):
    n_cols = 6 * D_MODEL
    return pl.pallas_call(
        _ada_kernel,
        grid=(DEPTH, n_cols // ADA_COLS),
        in_specs=[
            pl.BlockSpec((BATCH, D_MODEL), lambda i, j: (0, 0)),
            pl.BlockSpec((1, D_MODEL, ADA_COLS), lambda i, j: (i, 0, j)),
            pl.BlockSpec((1, 1, ADA_COLS), lambda i, j: (i, 0, j)),
        ],
        out_specs=pl.BlockSpec((1, BATCH, ADA_COLS), lambda i, j: (i, 0, j)),
        out_shape=jax.ShapeDtypeStruct((DEPTH, BATCH, n_cols), jnp.float32),
        compiler_params=pltpu.CompilerParams(
            dimension_semantics=("arbitrary", "arbitrary"),
            vmem_limit_bytes=VMEM_LIMIT_BYTES),
        name="ada_mods",
    )(c, w_ada, b_ada.reshape(DEPTH, 1, n_cols))


def _ssm_param_kernel(are_ref, aim_ref, ls_ref, btr_ref, bti_ref, ctr_ref, cti_ref,
                      wb_ref, wc_ref, a_ref):
    bf16 = jnp.bfloat16
    lr = jnp.minimum(are_ref[0], -1e-4)
    li = aim_ref[0]
    dt = jnp.exp(ls_ref[0])
    mag = jnp.exp(lr * dt)
    abr = mag * jnp.cos(li * dt)
    abi = mag * jnp.sin(li * dt)
    den = lr * lr + li * li
    qr = ((abr - 1.0) * lr + abi * li) / den
    qi = (abi * lr - (abr - 1.0) * li) / den
    a_ref[0, :, 0] = jnp.broadcast_to(abr[:, None, :], (N_CHUNKS, SUBLANES, LANES))
    a_ref[0, :, 1] = jnp.broadcast_to(abi[:, None, :], (N_CHUNKS, SUBLANES, LANES))

    btr = btr_ref[0]
    bti = bti_ref[0]
    bbr = qr[:, None, :] * btr - qi[:, None, :] * bti
    bbi = qr[:, None, :] * bti + qi[:, None, :] * btr

    b_row_group = lax.broadcasted_iota(jnp.int32, (MXU_DIM, LANES), 0) // SSM_GROUP
    b_lane_sub = lax.broadcasted_iota(jnp.int32, (MXU_DIM, LANES), 1) // SSM_STATE
    c_col_group = lax.broadcasted_iota(jnp.int32, (LANES, MXU_DIM), 1) // SSM_GROUP
    c_row_sub = lax.broadcasted_iota(jnp.int32, (LANES, MXU_DIM), 0) // SSM_STATE
    spread = (lax.broadcasted_iota(jnp.int32, (SSM_GROUP, MXU_DIM), 1) % SSM_GROUP
              == lax.broadcasted_iota(jnp.int32, (SSM_GROUP, MXU_DIM), 0)).astype(bf16)

    def b_tile(block, live):
        rep = jnp.broadcast_to(block[None], (GROUPS_PER_SLAB, SSM_GROUP, LANES))
        return jnp.where(live, rep.reshape(MXU_DIM, LANES), 0.0).astype(bf16)

    def c_tile(block, live):
        rep = jnp.dot(block.astype(bf16), spread, preferred_element_type=jnp.float32)
        return jnp.where(live, rep, 0.0).astype(bf16)

    for j in range(N_CHUNKS):
        slab, k = divmod(j, CHUNKS_PER_SLAB)
        b_live = b_row_group == k * GROUPS_PER_CHUNK + b_lane_sub
        wb_ref[0, j, :, pl.ds(0, LANES)] = b_tile(bbr[j], b_live)
        wb_ref[0, j, :, pl.ds(LANES, LANES)] = b_tile(bbi[j], b_live)
        c_live = c_col_group == k * GROUPS_PER_CHUNK + c_row_sub
        row = k * 2 * LANES
        wc_ref[0, slab, pl.ds(row, LANES), :] = c_tile(ctr_ref[0, j], c_live)
        wc_ref[0, slab, pl.ds(row + LANES, LANES), :] = c_tile(-cti_ref[0, j], c_live)


def _ssm_prep(a_re, a_im, log_step, b_re, b_im, c_re, c_im):
    n = a_re.shape[0]
    f32 = jnp.float32
    g, p, h = SSM_GROUPS, SSM_STATE, SSM_GROUP

    def chunk_lanes(v):
        return v.reshape(n, N_CHUNKS, LANES)

    def b_blocks(v):
        v = v.reshape(n, N_CHUNKS, GROUPS_PER_CHUNK, p, h)
        return jnp.transpose(v, (0, 1, 4, 2, 3)).reshape(n, N_CHUNKS, h, LANES)

    def c_blocks(v):
        v = v.reshape(n, N_CHUNKS, GROUPS_PER_CHUNK, h, p)
        return jnp.transpose(v, (0, 1, 2, 4, 3)).reshape(n, N_CHUNKS, LANES, h)

    ls = chunk_lanes(jnp.broadcast_to(log_step[:, :, None], (n, g, p)))
    lane_spec = pl.BlockSpec((1, N_CHUNKS, LANES), lambda i: (i, 0, 0))
    b_spec = pl.BlockSpec((1, N_CHUNKS, h, LANES), lambda i: (i, 0, 0, 0))
    c_spec = pl.BlockSpec((1, N_CHUNKS, LANES, h), lambda i: (i, 0, 0, 0))
    return pl.pallas_call(
        _ssm_param_kernel,
        grid=(n,),
        in_specs=[lane_spec, lane_spec, lane_spec, b_spec, b_spec, c_spec, c_spec],
        out_specs=(
            pl.BlockSpec((1, N_CHUNKS, MXU_DIM, 2 * LANES), lambda i: (i, 0, 0, 0)),
            pl.BlockSpec((1, N_SLABS, SLAB_STATE, MXU_DIM), lambda i: (i, 0, 0, 0)),
            pl.BlockSpec((1, N_CHUNKS, 2, SUBLANES, LANES), lambda i: (i, 0, 0, 0, 0)),
        ),
        out_shape=(
            jax.ShapeDtypeStruct((n, N_CHUNKS, MXU_DIM, 2 * LANES), jnp.bfloat16),
            jax.ShapeDtypeStruct((n, N_SLABS, SLAB_STATE, MXU_DIM), jnp.bfloat16),
            jax.ShapeDtypeStruct((n, N_CHUNKS, 2, SUBLANES, LANES), f32),
        ),
        compiler_params=pltpu.CompilerParams(
            dimension_semantics=("arbitrary",), vmem_limit_bytes=VMEM_LIMIT_BYTES),
        name="ssm_params",
    )(chunk_lanes(a_re), chunk_lanes(a_im), ls, b_blocks(b_re), b_blocks(b_im),
      c_blocks(c_re), c_blocks(c_im))


def _s5_kernel(x_ref, mods_ref, gain_ref, wb_ref, wc_ref, a_ref, d_ref, wout_ref,
               o_ref, bu_ref, xs_ref, state_ref, *, steps):
    @pl.when(pl.program_id(0) == 0)
    def _():
        state_ref[...] = jnp.zeros_like(state_ref)

    x = x_ref[...]
    mods = mods_ref[0]
    shift, scale, gate = (mods[:, k * D_MODEL:(k + 1) * D_MODEL] for k in range(3))
    h = _norm_modulate(x, gain_ref[0], shift, scale)
    hb = h.astype(jnp.bfloat16)

    ys = []
    for slab in range(N_SLABS):
        chunks = range(slab * CHUNKS_PER_SLAB, (slab + 1) * CHUNKS_PER_SLAB)
        u = hb[:, slab * MXU_DIM:(slab + 1) * MXU_DIM]
        for j in chunks:
            bu_ref[j] = jnp.dot(u, wb_ref[0, j], preferred_element_type=jnp.float32)

        carry = [(state_ref[j, 0], state_ref[j, 1]) for j in chunks]
        for t0 in range(0, steps, BF16_ROWS // SUBLANES):
            for k, j in enumerate(chunks):
                xr, xi = carry[k]
                a_re, a_im = a_ref[0, j, 0], a_ref[0, j, 1]
                outs_r, outs_i = [], []
                for t in range(t0, t0 + BF16_ROWS // SUBLANES):
                    rows = pl.ds(t * SUBLANES, SUBLANES)
                    bu_r = bu_ref[j, rows, pl.ds(0, LANES)]
                    bu_i = bu_ref[j, rows, pl.ds(LANES, LANES)]
                    xr, xi = (a_re * xr - a_im * xi + bu_r, a_re * xi + a_im * xr + bu_i)
                    outs_r.append(xr)
                    outs_i.append(xi)
                carry[k] = (xr, xi)
                rows = pl.ds(t0 * SUBLANES, BF16_ROWS)
                col = k * 2 * LANES
                xs_ref[slab, rows, pl.ds(col, LANES)] = (
                    jnp.concatenate(outs_r, axis=0).astype(jnp.bfloat16))
                xs_ref[slab, rows, pl.ds(col + LANES, LANES)] = (
                    jnp.concatenate(outs_i, axis=0).astype(jnp.bfloat16))
        for k, j in enumerate(chunks):
            state_ref[j, 0] = carry[k][0]
            state_ref[j, 1] = carry[k][1]

        ys.append(jnp.dot(xs_ref[slab], wc_ref[0, slab], preferred_element_type=jnp.float32))

    y = jnp.concatenate(ys, axis=-1) + d_ref[0] * h
    y = jax.nn.gelu(y).astype(jnp.bfloat16)
    z = jnp.dot(y, wout_ref[0], preferred_element_type=jnp.float32)
    mix = z[:, :D_MODEL] * jax.nn.sigmoid(z[:, D_MODEL:])
    o_ref[...] = _gated_residual(x, gate, mix)


def _s5_layer(x2, mods, layer, j, gain, wb, wc, a, d, w_out):
    rows = x2.shape[0]
    steps = ROWS_MIX // BATCH
    return pl.pallas_call(
        functools.partial(_s5_kernel, steps=steps),
        grid=(rows // ROWS_MIX,),
        in_specs=[
            pl.BlockSpec((ROWS_MIX, D_MODEL), lambda i: (i, 0)),
            pl.BlockSpec((1, BATCH, 6 * D_MODEL), lambda i: (layer, 0, 0)),
            pl.BlockSpec((1, 1, D_MODEL), lambda i: (layer, 0, 0)),
            _layer_spec(wb.shape, j),
            _layer_spec(wc.shape, j),
            _layer_spec(a.shape, j),
            _layer_spec(d.shape, j),
            _layer_spec(w_out.shape, j),
        ],
        out_specs=pl.BlockSpec((ROWS_MIX, D_MODEL), lambda i: (i, 0)),
        out_shape=jax.ShapeDtypeStruct(x2.shape, x2.dtype),
        scratch_shapes=[
            pltpu.VMEM((N_CHUNKS, ROWS_MIX, 2 * LANES), jnp.float32),
            pltpu.VMEM((N_SLABS, ROWS_MIX, SLAB_STATE), jnp.bfloat16),
            pltpu.VMEM((N_CHUNKS, 2, SUBLANES, LANES), jnp.float32),
        ],
        compiler_params=pltpu.CompilerParams(
            dimension_semantics=("arbitrary",), vmem_limit_bytes=VMEM_LIMIT_BYTES),
        name="s5_layer",
    )(x2, mods, gain, wb, wc, a, d, w_out)


def _conv_kernel(x_ref, mods_ref, gain_ref, win_ref, cw_ref, wout_ref, o_ref, cv_ref):
    rows = x_ref.shape[0]
    hist = (CONV_WIDTH - 1) * BATCH

    @pl.when(pl.program_id(0) == 0)
    def _():
        cv_ref[pl.ds(0, hist), :] = jnp.zeros((hist, D_MODEL), jnp.float32)

    x = x_ref[...]
    mods = mods_ref[0]
    shift, scale, gate = (mods[:, k * D_MODEL:(k + 1) * D_MODEL] for k in range(3))
    h = _norm_modulate(x, gain_ref[0], shift, scale)
    proj = jnp.dot(h.astype(jnp.bfloat16), win_ref[0], preferred_element_type=jnp.float32)
    bg = proj[:, :D_MODEL]
    cv = proj[:, D_MODEL:2 * D_MODEL] * proj[:, 2 * D_MODEL:]
    cv_ref[pl.ds(hist, rows), :] = cv
    cw = cw_ref[0]
    conv = (cw[2:3] * cv + cw[1:2] * cv_ref[pl.ds(BATCH, rows), :]
            + cw[0:1] * cv_ref[pl.ds(0, rows), :])
    cv_ref[pl.ds(0, hist), :] = cv_ref[pl.ds(rows, hist), :]
    out = jnp.dot((bg * conv).astype(jnp.bfloat16), wout_ref[0],
                  preferred_element_type=jnp.float32)
    o_ref[...] = _gated_residual(x, gate, out)


def _conv_layer(x2, mods, layer, j, gain, w_in, conv_w, w_out):
    rows = x2.shape[0]
    hist = (CONV_WIDTH - 1) * BATCH
    return pl.pallas_call(
        _conv_kernel,
        grid=(rows // ROWS_MIX,),
        in_specs=[
            pl.BlockSpec((ROWS_MIX, D_MODEL), lambda i: (i, 0)),
            pl.BlockSpec((1, BATCH, 6 * D_MODEL), lambda i: (layer, 0, 0)),
            pl.BlockSpec((1, 1, D_MODEL), lambda i: (layer, 0, 0)),
            _layer_spec(w_in.shape, j),
            _layer_spec(conv_w.shape, j),
            _layer_spec(w_out.shape, j),
        ],
        out_specs=pl.BlockSpec((ROWS_MIX, D_MODEL), lambda i: (i, 0)),
        out_shape=jax.ShapeDtypeStruct(x2.shape, x2.dtype),
        scratch_shapes=[pltpu.VMEM((ROWS_MIX + hist, D_MODEL), jnp.float32)],
        compiler_params=pltpu.CompilerParams(
            dimension_semantics=("arbitrary",), vmem_limit_bytes=VMEM_LIMIT_BYTES),
        name="conv_layer",
    )(x2, mods, gain, w_in, conv_w, w_out)


def _ffn_kernel(x_ref, mods_ref, gain_ref, win_ref, wout_ref, fg_ref, o_ref, *, final_norm):
    hidden = wout_ref.shape[1]
    x = x_ref[...]
    mods = mods_ref[0]
    shift, scale, gate = (mods[:, k * D_MODEL:(k + 1) * D_MODEL] for k in range(3, 6))
    h = _norm_modulate(x, gain_ref[0], shift, scale)
    gu = jnp.dot(h.astype(jnp.bfloat16), win_ref[0], preferred_element_type=jnp.float32)
    act = (jax.nn.silu(gu[:, :hidden]) * gu[:, hidden:]).astype(jnp.bfloat16)
    out = jnp.dot(act, wout_ref[0], preferred_element_type=jnp.float32)
    xn = _gated_residual(x, gate, out)
    if final_norm:
        xn = xn * lax.rsqrt(jnp.mean(xn * xn, axis=-1, keepdims=True) + RMS_EPS) * fg_ref[...]
    o_ref[...] = xn


def _ffn_layer(x2, mods, layer, gain, w_in, w_out, final_g, final_norm):
    rows = x2.shape[0]
    return pl.pallas_call(
        functools.partial(_ffn_kernel, final_norm=final_norm),
        grid=(rows // ROWS_FFN,),
        in_specs=[
            pl.BlockSpec((ROWS_FFN, D_MODEL), lambda i: (i, 0)),
            pl.BlockSpec((1, BATCH, 6 * D_MODEL), lambda i: (layer, 0, 0)),
            pl.BlockSpec((1, 1, D_MODEL), lambda i: (layer, 0, 0)),
            _layer_spec(w_in.shape, layer),
            _layer_spec(w_out.shape, layer),
            _const_spec((1, D_MODEL)),
        ],
        out_specs=pl.BlockSpec((ROWS_FFN, D_MODEL), lambda i: (i, 0)),
        out_shape=jax.ShapeDtypeStruct(x2.shape, x2.dtype),
        compiler_params=pltpu.CompilerParams(
            dimension_semantics=("arbitrary",), vmem_limit_bytes=VMEM_LIMIT_BYTES),
        name="ffn_layer",
    )(x2, mods, gain, w_in, w_out, final_g.reshape(1, D_MODEL))


def kernel(x, c, norm1_g, norm2_g, w_ada, b_ada, ssm_a_re, ssm_a_im, ssm_log_step, ssm_b_re, ssm_b_im, ssm_c_re, ssm_c_im, ssm_d, ssm_w_out, conv_w_in, conv_w, conv_w_out, w_ffn_in, w_ffn_out, final_g):
    bsz, seqlen, _ = x.shape
    bf16 = jnp.bfloat16
    mods = _ada_mods(c, w_ada, b_ada)
    gain1 = norm1_g.reshape(DEPTH, 1, D_MODEL)
    gain2 = norm2_g.reshape(DEPTH, 1, D_MODEL)
    wb, wc, a = _ssm_prep(ssm_a_re, ssm_a_im, ssm_log_step, ssm_b_re, ssm_b_im,
                          ssm_c_re, ssm_c_im)
    ssm_d3 = ssm_d.reshape(-1, 1, D_MODEL)
    ssm_w_out, conv_w_in, conv_w_out = (w.astype(bf16) for w in (ssm_w_out, conv_w_in, conv_w_out))
    w_ffn_in, w_ffn_out = w_ffn_in.astype(bf16), w_ffn_out.astype(bf16)
    x2 = jnp.transpose(x, (1, 0, 2)).reshape(seqlen * bsz, D_MODEL)
    for i in range(DEPTH):
        j = i // N_MIXERS
        if i % N_MIXERS == 0:
            x2 = _s5_layer(x2, mods, i, j, gain1, wb, wc, a, ssm_d3, ssm_w_out)
        else:
            x2 = _conv_layer(x2, mods, i, j, gain1, conv_w_in, conv_w, conv_w_out)
        x2 = _ffn_layer(x2, mods, i, gain2, w_ffn_in, w_ffn_out, final_g,
                        final_norm=(i == DEPTH - 1))
    return jnp.transpose(x2.reshape(seqlen, bsz, D_MODEL), (1, 0, 2))
```

```python
import functools

import jax
import jax.numpy as jnp
from jax import lax
from jax.experimental import pallas as pl
from jax.experimental.pallas import tpu as pltpu

D_MODEL = 1024
BATCH = 8
DEPTH = 4
N_MIXERS = 2
SSM_GROUP = 16
SSM_GROUPS = D_MODEL // SSM_GROUP
SSM_STATE = 64
CONV_WIDTH = 3
RMS_EPS = 1e-6

LANES = 128
SUBLANES = 8
BF16_ROWS = 16
MXU_DIM = 256
VMEM_LIMIT_BYTES = 56 * 1024 * 1024

N_CHUNKS = SSM_GROUPS * SSM_STATE // LANES
GROUPS_PER_CHUNK = LANES // SSM_STATE
N_SLABS = D_MODEL // MXU_DIM
GROUPS_PER_SLAB = MXU_DIM // SSM_GROUP
CHUNKS_PER_SLAB = N_CHUNKS // N_SLABS
SLAB_STATE = CHUNKS_PER_SLAB * 2 * LANES

ROWS_FFN = 512
ROWS_MIX = 512
ADA_COLS = 1536


def _const_spec(shape, index=None):
    index = (0,) * len(shape) if index is None else index
    return pl.BlockSpec(shape, lambda *_: index, pipeline_mode=pl.Buffered(1))


def _layer_spec(shape, layer):
    return _const_spec((1,) + tuple(shape[1:]), (layer,) + (0,) * (len(shape) - 1))


def _norm_modulate(x, gain, shift, scale):
    rows = x.shape[0]
    y = x * lax.rsqrt(jnp.mean(x * x, axis=-1, keepdims=True) + RMS_EPS)
    y = (y * gain).reshape(rows // BATCH, BATCH, D_MODEL)
    return (y * (1.0 + scale)[None] + shift[None]).reshape(rows, D_MODEL)


def _gated_residual(x, gate, update):
    rows = x.shape[0]
    upd = update.reshape(rows // BATCH, BATCH, D_MODEL) * gate[None]
    return x + upd.reshape(rows, D_MODEL)


def _batch_row_copies(hbm, buf, sems, block, slot, to_hbm):
    steps = buf.shape[1]
    copies = []
    for b in range(BATCH):
        hbm_rows = hbm.at[b, pl.ds(block * steps, steps), :]
        vmem_rows = buf.at[slot, :, b, :]
        src, dst = (vmem_rows, hbm_rows) if to_hbm else (hbm_rows, vmem_rows)
        copies.append(pltpu.make_async_copy(src, dst, sems.at[slot, b]))
    return copies


def _fetch_time_major(x_hbm, buf, sems, n_blocks):
    i = pl.program_id(0)

    @pl.when(i == 0)
    def _():
        for c in _batch_row_copies(x_hbm, buf, sems, 0, 0, to_hbm=False):
            c.start()

    @pl.when(i + 1 < n_blocks)
    def _():
        for c in _batch_row_copies(x_hbm, buf, sems, i + 1, (i + 1) % 2, to_hbm=False):
            c.start()

    for c in _batch_row_copies(x_hbm, buf, sems, i, i % 2, to_hbm=False):
        c.wait()
    return buf[i % 2].reshape(buf.shape[1] * BATCH, D_MODEL)


def _store_batch_major(value, o_hbm, buf, sems, n_blocks):
    i = pl.program_id(0)
    slot = i % 2

    @pl.when(i >= 2)
    def _():
        for c in _batch_row_copies(o_hbm, buf, sems, i - 2, slot, to_hbm=True):
            c.wait()

    buf[slot] = value.reshape(buf.shape[1], BATCH, D_MODEL)
    for c in _batch_row_copies(o_hbm, buf, sems, i, slot, to_hbm=True):
        c.start()

    @pl.when(i == n_blocks - 1)
    def _():
        for back in range(min(2, n_blocks)):
            for c in _batch_row_copies(o_hbm, buf, sems, i - back, (i - back) % 2, to_hbm=True):
                c.wait()


def _ada_kernel(c_ref, w_ref, b_ref, o_ref):
    c_act = jax.nn.silu(c_ref[...])
    o_ref[0] = jnp.dot(c_act, w_ref[0], preferred_element_type=jnp.float32) + b_ref[0]


def _ada_mods(c, w_ada, b_ada):
    n_cols = 6 * D_MODEL
    return pl.pallas_call(
        _ada_kernel,
        grid=(DEPTH, n_cols // ADA_COLS),
        in_specs=[
            pl.BlockSpec((BATCH, D_MODEL), lambda i, j: (0, 0)),
            pl.BlockSpec((1, D_MODEL, ADA_COLS), lambda i, j: (i, 0, j)),
            pl.BlockSpec((1, 1, ADA_COLS), lambda i, j: (i, 0, j)),
        ],
        out_specs=pl.BlockSpec((1, BATCH, ADA_COLS), lambda i, j: (i, 0, j)),
        out_shape=jax.ShapeDtypeStruct((DEPTH, BATCH, n_cols), jnp.float32),
        compiler_params=pltpu.CompilerParams(
            dimension_semantics=("arbitrary", "arbitrary"),
            vmem_limit_bytes=VMEM_LIMIT_BYTES),
        name="ada_mods",
    )(c, w_ada, b_ada.reshape(DEPTH, 1, n_cols))


def _ssm_param_kernel(are_ref, aim_ref, ls_ref, btr_ref, bti_ref, ctr_ref, cti_ref,
                      wb_ref, wc_ref, a_ref):
    bf16 = jnp.bfloat16
    lr = jnp.minimum(are_ref[0], -1e-4)
    li = aim_ref[0]
    dt = jnp.exp(ls_ref[0])
    mag = jnp.exp(lr * dt)
    abr = mag * jnp.cos(li * dt)
    abi = mag * jnp.sin(li * dt)
    den = lr * lr + li * li
    qr = ((abr - 1.0) * lr + abi * li) / den
    qi = (abi * lr - (abr - 1.0) * li) / den
    a_ref[0, :, 0] = jnp.broadcast_to(abr[:, None, :], (N_CHUNKS, SUBLANES, LANES))
    a_ref[0, :, 1] = jnp.broadcast_to(abi[:, None, :], (N_CHUNKS, SUBLANES, LANES))

    btr = btr_ref[0]
    bti = bti_ref[0]
    bbr = qr[:, None, :] * btr - qi[:, None, :] * bti
    bbi = qr[:, None, :] * bti + qi[:, None, :] * btr

    b_row_group = lax.broadcasted_iota(jnp.int32, (MXU_DIM, LANES), 0) // SSM_GROUP
    b_lane_sub = lax.broadcasted_iota(jnp.int32, (MXU_DIM, LANES), 1) // SSM_STATE
    c_col_group = lax.broadcasted_iota(jnp.int32, (LANES, MXU_DIM), 1) // SSM_GROUP
    c_row_sub = lax.broadcasted_iota(jnp.int32, (LANES, MXU_DIM), 0) // SSM_STATE
    spread = (lax.broadcasted_iota(jnp.int32, (SSM_GROUP, MXU_DIM), 1) % SSM_GROUP
              == lax.broadcasted_iota(jnp.int32, (SSM_GROUP, MXU_DIM), 0)).astype(bf16)

    def b_tile(block, live):
        rep = jnp.broadcast_to(block[None], (GROUPS_PER_SLAB, SSM_GROUP, LANES))
        return jnp.where(live, rep.reshape(MXU_DIM, LANES), 0.0).astype(bf16)

    def c_tile(block, live):
        rep = jnp.dot(block.astype(bf16), spread, preferred_element_type=jnp.float32)
        return jnp.where(live, rep, 0.0).astype(bf16)

    for j in range(N_CHUNKS):
        slab, k = divmod(j, CHUNKS_PER_SLAB)
        b_live = b_row_group == k * GROUPS_PER_CHUNK + b_lane_sub
        wb_ref[0, j, :, pl.ds(0, LANES)] = b_tile(bbr[j], b_live)
        wb_ref[0, j, :, pl.ds(LANES, LANES)] = b_tile(bbi[j], b_live)
        c_live = c_col_group == k * GROUPS_PER_CHUNK + c_row_sub
        row = k * 2 * LANES
        wc_ref[0, slab, pl.ds(row, LANES), :] = c_tile(ctr_ref[0, j], c_live)
        wc_ref[0, slab, pl.ds(row + LANES, LANES), :] = c_tile(-cti_ref[0, j], c_live)


def _ssm_prep(a_re, a_im, log_step, b_re, b_im, c_re, c_im):
    n = a_re.shape[0]
    f32 = jnp.float32
    g, p, h = SSM_GROUPS, SSM_STATE, SSM_GROUP

    def chunk_lanes(v):
        return v.reshape(n, N_CHUNKS, LANES)

    def b_blocks(v):
        v = v.reshape(n, N_CHUNKS, GROUPS_PER_CHUNK, p, h)
        return jnp.transpose(v, (0, 1, 4, 2, 3)).reshape(n, N_CHUNKS, h, LANES)

    def c_blocks(v):
        v = v.reshape(n, N_CHUNKS, GROUPS_PER_CHUNK, h, p)
        return jnp.transpose(v, (0, 1, 2, 4, 3)).reshape(n, N_CHUNKS, LANES, h)

    ls = chunk_lanes(jnp.broadcast_to(log_step[:, :, None], (n, g, p)))
    lane_spec = pl.BlockSpec((1, N_CHUNKS, LANES), lambda i: (i, 0, 0))
    b_spec = pl.BlockSpec((1, N_CHUNKS, h, LANES), lambda i: (i, 0, 0, 0))
    c_spec = pl.BlockSpec((1, N_CHUNKS, LANES, h), lambda i: (i, 0, 0, 0))
    return pl.pallas_call(
        _ssm_param_kernel,
        grid=(n,),
        in_specs=[lane_spec, lane_spec, lane_spec, b_spec, b_spec, c_spec, c_spec],
        out_specs=(
            pl.BlockSpec((1, N_CHUNKS, MXU_DIM, 2 * LANES), lambda i: (i, 0, 0, 0)),
            pl.BlockSpec((1, N_SLABS, SLAB_STATE, MXU_DIM), lambda i: (i, 0, 0, 0)),
            pl.BlockSpec((1, N_CHUNKS, 2, SUBLANES, LANES), lambda i: (i, 0, 0, 0, 0)),
        ),
        out_shape=(
            jax.ShapeDtypeStruct((n, N_CHUNKS, MXU_DIM, 2 * LANES), jnp.bfloat16),
            jax.ShapeDtypeStruct((n, N_SLABS, SLAB_STATE, MXU_DIM), jnp.bfloat16),
            jax.ShapeDtypeStruct((n, N_CHUNKS, 2, SUBLANES, LANES), f32),
        ),
        compiler_params=pltpu.CompilerParams(
            dimension_semantics=("arbitrary",), vmem_limit_bytes=VMEM_LIMIT_BYTES),
        name="ssm_params",
    )(chunk_lanes(a_re), chunk_lanes(a_im), ls, b_blocks(b_re), b_blocks(b_im),
      c_blocks(c_re), c_blocks(c_im))


def _s5_kernel(x_ref, mods_ref, gain_ref, wb_ref, wc_ref, a_ref, d_ref, wout_ref,
               o_ref, bu_ref, xs_ref, state_ref, *fetch_scratch, steps, n_blocks):
    @pl.when(pl.program_id(0) == 0)
    def _():
        state_ref[...] = jnp.zeros_like(state_ref)

    if fetch_scratch:
        x = _fetch_time_major(x_ref, *fetch_scratch, n_blocks)
    else:
        x = x_ref[...]
    mods = mods_ref[0]
    shift, scale, gate = (mods[:, k * D_MODEL:(k + 1) * D_MODEL] for k in range(3))
    h = _norm_modulate(x, gain_ref[0], shift, scale)
    hb = h.astype(jnp.bfloat16)

    ys = []
    for slab in range(N_SLABS):
        chunks = range(slab * CHUNKS_PER_SLAB, (slab + 1) * CHUNKS_PER_SLAB)
        u = hb[:, slab * MXU_DIM:(slab + 1) * MXU_DIM]
        for j in chunks:
            bu_ref[j] = jnp.dot(u, wb_ref[0, j], preferred_element_type=jnp.float32)

        carry = [(state_ref[j, 0], state_ref[j, 1]) for j in chunks]
        for t0 in range(0, steps, BF16_ROWS // SUBLANES):
            for k, j in enumerate(chunks):
                xr, xi = carry[k]
                a_re, a_im = a_ref[0, j, 0], a_ref[0, j, 1]
                outs_r, outs_i = [], []
                for t in range(t0, t0 + BF16_ROWS // SUBLANES):
                    rows = pl.ds(t * SUBLANES, SUBLANES)
                    bu_r = bu_ref[j, rows, pl.ds(0, LANES)]
                    bu_i = bu_ref[j, rows, pl.ds(LANES, LANES)]
                    xr, xi = (a_re * xr - a_im * xi + bu_r, a_re * xi + a_im * xr + bu_i)
                    outs_r.append(xr)
                    outs_i.append(xi)
                carry[k] = (xr, xi)
                rows = pl.ds(t0 * SUBLANES, BF16_ROWS)
                col = k * 2 * LANES
                xs_ref[slab, rows, pl.ds(col, LANES)] = (
                    jnp.concatenate(outs_r, axis=0).astype(jnp.bfloat16))
                xs_ref[slab, rows, pl.ds(col + LANES, LANES)] = (
                    jnp.concatenate(outs_i, axis=0).astype(jnp.bfloat16))
        for k, j in enumerate(chunks):
            state_ref[j, 0] = carry[k][0]
            state_ref[j, 1] = carry[k][1]

        ys.append(jnp.dot(xs_ref[slab], wc_ref[0, slab], preferred_element_type=jnp.float32))

    y = jnp.concatenate(ys, axis=-1) + d_ref[0] * h
    y = jax.nn.gelu(y).astype(jnp.bfloat16)
    z = jnp.dot(y, wout_ref[0], preferred_element_type=jnp.float32)
    mix = z[:, :D_MODEL] * jax.nn.sigmoid(z[:, D_MODEL:])
    o_ref[...] = _gated_residual(x, gate, mix)


def _s5_layer(x_in, mods, layer, j, gain, wb, wc, a, d, w_out):
    steps = ROWS_MIX // BATCH
    batch_major = x_in.ndim == 3
    rows = x_in.shape[0] * x_in.shape[1] if batch_major else x_in.shape[0]
    n_blocks = rows // ROWS_MIX
    if batch_major:
        x_spec = pl.BlockSpec(memory_space=pl.ANY)
        fetch_scratch = [pltpu.VMEM((2, steps, BATCH, D_MODEL), jnp.float32),
                         pltpu.SemaphoreType.DMA((2, BATCH))]
    else:
        x_spec = pl.BlockSpec((ROWS_MIX, D_MODEL), lambda i: (i, 0))
        fetch_scratch = []
    return pl.pallas_call(
        functools.partial(_s5_kernel, steps=steps, n_blocks=n_blocks),
        grid=(n_blocks,),
        in_specs=[
            x_spec,
            pl.BlockSpec((1, BATCH, 6 * D_MODEL), lambda i: (layer, 0, 0)),
            pl.BlockSpec((1, 1, D_MODEL), lambda i: (layer, 0, 0)),
            _layer_spec(wb.shape, j),
            _layer_spec(wc.shape, j),
            _layer_spec(a.shape, j),
            _layer_spec(d.shape, j),
            _layer_spec(w_out.shape, j),
        ],
        out_specs=pl.BlockSpec((ROWS_MIX, D_MODEL), lambda i: (i, 0)),
        out_shape=jax.ShapeDtypeStruct((rows, D_MODEL), x_in.dtype),
        scratch_shapes=[
            pltpu.VMEM((N_CHUNKS, ROWS_MIX, 2 * LANES), jnp.float32),
            pltpu.VMEM((N_SLABS, ROWS_MIX, SLAB_STATE), jnp.bfloat16),
            pltpu.VMEM((N_CHUNKS, 2, SUBLANES, LANES), jnp.float32),
        ] + fetch_scratch,
        compiler_params=pltpu.CompilerParams(
            dimension_semantics=("arbitrary",), vmem_limit_bytes=VMEM_LIMIT_BYTES),
        name="s5_layer",
    )(x_in, mods, gain, wb, wc, a, d, w_out)


def _conv_kernel(x_ref, mods_ref, gain_ref, win_ref, cw_ref, wout_ref, o_ref, cv_ref):
    rows = x_ref.shape[0]
    hist = (CONV_WIDTH - 1) * BATCH

    @pl.when(pl.program_id(0) == 0)
    def _():
        cv_ref[pl.ds(0, hist), :] = jnp.zeros((hist, D_MODEL), jnp.float32)

    x = x_ref[...]
    mods = mods_ref[0]
    shift, scale, gate = (mods[:, k * D_MODEL:(k + 1) * D_MODEL] for k in range(3))
    h = _norm_modulate(x, gain_ref[0], shift, scale)
    proj = jnp.dot(h.astype(jnp.bfloat16), win_ref[0], preferred_element_type=jnp.float32)
    bg = proj[:, :D_MODEL]
    cv = proj[:, D_MODEL:2 * D_MODEL] * proj[:, 2 * D_MODEL:]
    cv_ref[pl.ds(hist, rows), :] = cv
    cw = cw_ref[0]
    conv = (cw[2:3] * cv + cw[1:2] * cv_ref[pl.ds(BATCH, rows), :]
            + cw[0:1] * cv_ref[pl.ds(0, rows), :])
    cv_ref[pl.ds(0, hist), :] = cv_ref[pl.ds(rows, hist), :]
    out = jnp.dot((bg * conv).astype(jnp.bfloat16), wout_ref[0],
                  preferred_element_type=jnp.float32)
    o_ref[...] = _gated_residual(x, gate, out)


def _conv_layer(x2, mods, layer, j, gain, w_in, conv_w, w_out):
    rows = x2.shape[0]
    hist = (CONV_WIDTH - 1) * BATCH
    return pl.pallas_call(
        _conv_kernel,
        grid=(rows // ROWS_MIX,),
        in_specs=[
            pl.BlockSpec((ROWS_MIX, D_MODEL), lambda i: (i, 0)),
            pl.BlockSpec((1, BATCH, 6 * D_MODEL), lambda i: (layer, 0, 0)),
            pl.BlockSpec((1, 1, D_MODEL), lambda i: (layer, 0, 0)),
            _layer_spec(w_in.shape, j),
            _layer_spec(conv_w.shape, j),
            _layer_spec(w_out.shape, j),
        ],
        out_specs=pl.BlockSpec((ROWS_MIX, D_MODEL), lambda i: (i, 0)),
        out_shape=jax.ShapeDtypeStruct(x2.shape, x2.dtype),
        scratch_shapes=[pltpu.VMEM((ROWS_MIX + hist, D_MODEL), jnp.float32)],
        compiler_params=pltpu.CompilerParams(
            dimension_semantics=("arbitrary",), vmem_limit_bytes=VMEM_LIMIT_BYTES),
        name="conv_layer",
    )(x2, mods, gain, w_in, conv_w, w_out)


def _ffn_kernel(x_ref, mods_ref, gain_ref, win_ref, wout_ref, fg_ref, o_ref, *store_scratch,
                final_norm, n_blocks):
    hidden = wout_ref.shape[1]
    x = x_ref[...]
    mods = mods_ref[0]
    shift, scale, gate = (mods[:, k * D_MODEL:(k + 1) * D_MODEL] for k in range(3, 6))
    h = _norm_modulate(x, gain_ref[0], shift, scale)
    gu = jnp.dot(h.astype(jnp.bfloat16), win_ref[0], preferred_element_type=jnp.float32)
    act = (jax.nn.silu(gu[:, :hidden]) * gu[:, hidden:]).astype(jnp.bfloat16)
    out = jnp.dot(act, wout_ref[0], preferred_element_type=jnp.float32)
    xn = _gated_residual(x, gate, out)
    if final_norm:
        xn = xn * lax.rsqrt(jnp.mean(xn * xn, axis=-1, keepdims=True) + RMS_EPS) * fg_ref[...]
    if store_scratch:
        _store_batch_major(xn, o_ref, *store_scratch, n_blocks)
    else:
        o_ref[...] = xn


def _ffn_layer(x2, mods, layer, gain, w_in, w_out, final_g, final_norm, batch_major_out):
    rows = x2.shape[0]
    n_blocks = rows // ROWS_FFN
    if batch_major_out:
        steps = ROWS_FFN // BATCH
        out_spec = pl.BlockSpec(memory_space=pl.ANY)
        out_shape = jax.ShapeDtypeStruct((BATCH, rows // BATCH, D_MODEL), x2.dtype)
        store_scratch = [pltpu.VMEM((2, steps, BATCH, D_MODEL), jnp.float32),
                         pltpu.SemaphoreType.DMA((2, BATCH))]
    else:
        out_spec = pl.BlockSpec((ROWS_FFN, D_MODEL), lambda i: (i, 0))
        out_shape = jax.ShapeDtypeStruct(x2.shape, x2.dtype)
        store_scratch = []
    return pl.pallas_call(
        functools.partial(_ffn_kernel, final_norm=final_norm, n_blocks=n_blocks),
        grid=(n_blocks,),
        in_specs=[
            pl.BlockSpec((ROWS_FFN, D_MODEL), lambda i: (i, 0)),
            pl.BlockSpec((1, BATCH, 6 * D_MODEL), lambda i: (layer, 0, 0)),
            pl.BlockSpec((1, 1, D_MODEL), lambda i: (layer, 0, 0)),
            _layer_spec(w_in.shape, layer),
            _layer_spec(w_out.shape, layer),
            _const_spec((1, D_MODEL)),
        ],
        out_specs=out_spec,
        out_shape=out_shape,
        scratch_shapes=store_scratch,
        compiler_params=pltpu.CompilerParams(
            dimension_semantics=("arbitrary",), vmem_limit_bytes=VMEM_LIMIT_BYTES),
        name="ffn_layer",
    )(x2, mods, gain, w_in, w_out, final_g.reshape(1, D_MODEL))


def kernel(x, c, norm1_g, norm2_g, w_ada, b_ada, ssm_a_re, ssm_a_im, ssm_log_step, ssm_b_re, ssm_b_im, ssm_c_re, ssm_c_im, ssm_d, ssm_w_out, conv_w_in, conv_w, conv_w_out, w_ffn_in, w_ffn_out, final_g):
    bf16 = jnp.bfloat16
    mods = _ada_mods(c, w_ada, b_ada)
    gain1 = norm1_g.reshape(DEPTH, 1, D_MODEL)
    gain2 = norm2_g.reshape(DEPTH, 1, D_MODEL)
    wb, wc, a = _ssm_prep(ssm_a_re, ssm_a_im, ssm_log_step, ssm_b_re, ssm_b_im,
                          ssm_c_re, ssm_c_im)
    ssm_d3 = ssm_d.reshape(-1, 1, D_MODEL)
    ssm_w_out, conv_w_in, conv_w_out = (w.astype(bf16) for w in (ssm_w_out, conv_w_in, conv_w_out))
    w_ffn_in, w_ffn_out = w_ffn_in.astype(bf16), w_ffn_out.astype(bf16)
    x2 = x
    for i in range(DEPTH):
        j = i // N_MIXERS
        if i % N_MIXERS == 0:
            x2 = _s5_layer(x2, mods, i, j, gain1, wb, wc, a, ssm_d3, ssm_w_out)
        else:
            x2 = _conv_layer(x2, mods, i, j, gain1, conv_w_in, conv_w, conv_w_out)
        last = i == DEPTH - 1
        x2 = _ffn_layer(x2, mods, i, gain2, w_ffn_in, w_ffn_out, final_g,
                        final_norm=last, batch_major_out=last)
    return x2
```

```python
import functools

import jax
import jax.numpy as jnp
from jax import lax
from jax.experimental import pallas as pl
from jax.experimental.pallas import tpu as pltpu

D_MODEL = 1024
BATCH = 8
DEPTH = 4
N_MIXERS = 2
SSM_GROUP = 16
SSM_GROUPS = D_MODEL // SSM_GROUP
SSM_STATE = 64
CONV_WIDTH = 3
RMS_EPS = 1e-6

LANES = 128
SUBLANES = 8
BF16_ROWS = 16
MXU_DIM = 256
VMEM_LIMIT_BYTES = 56 * 1024 * 1024
STAGE_BYTES = 3 * 512 * 1024

N_CHUNKS = SSM_GROUPS * SSM_STATE // LANES
GROUPS_PER_CHUNK = LANES // SSM_STATE
N_SLABS = D_MODEL // MXU_DIM
GROUPS_PER_SLAB = MXU_DIM // SSM_GROUP
CHUNKS_PER_SLAB = N_CHUNKS // N_SLABS
SLAB_STATE = CHUNKS_PER_SLAB * 2 * LANES

ROWS_FFN = 512
ROWS_MIX = 512
ADA_COLS = 1536


def _const_spec(shape, index=None):
    index = (0,) * len(shape) if index is None else index
    return pl.BlockSpec(shape, lambda *_: index, pipeline_mode=pl.Buffered(1))


def _layer_spec(shape, layer):
    return _const_spec((1,) + tuple(shape[1:]), (layer,) + (0,) * (len(shape) - 1))


def _norm_modulate(x, gain, shift, scale):
    rows = x.shape[0]
    y = x * lax.rsqrt(jnp.mean(x * x, axis=-1, keepdims=True) + RMS_EPS)
    y = y.reshape(rows // BATCH, BATCH, D_MODEL)
    return (y * (gain * (1.0 + scale))[None] + shift[None]).reshape(rows, D_MODEL)


def _gated_residual(x, gate, update):
    rows = x.shape[0]
    upd = update.reshape(rows // BATCH, BATCH, D_MODEL) * gate[None]
    return x + upd.reshape(rows, D_MODEL)


def _stage_scratch(shape):
    k_dim, n_dim = shape[-2:]
    fits = [r for r in range(BF16_ROWS, k_dim + 1, BF16_ROWS)
            if k_dim % r == 0 and r * n_dim * 4 <= STAGE_BYTES]
    return [pltpu.VMEM((k_dim, n_dim), jnp.bfloat16),
            pltpu.VMEM((2, max(fits), n_dim), jnp.float32),
            pltpu.SemaphoreType.DMA((2,))]


def _stage_bf16(w_hbm, layer, w_bf16, stage, sems):
    chunk = stage.shape[1]
    n_chunks = w_bf16.shape[0] // chunk

    def copy(k):
        return pltpu.make_async_copy(w_hbm.at[layer, pl.ds(k * chunk, chunk), :],
                                     stage.at[k % 2], sems.at[k % 2])

    copy(0).start()
    for k in range(n_chunks):
        if k + 1 < n_chunks:
            copy(k + 1).start()
        copy(k).wait()
        w_bf16[pl.ds(k * chunk, chunk), :] = stage[k % 2].astype(jnp.bfloat16)


def _batch_row_copies(hbm, buf, sems, block, slot, to_hbm):
    steps = buf.shape[1]
    copies = []
    for b in range(BATCH):
        hbm_rows = hbm.at[b, pl.ds(block * steps, steps), :]
        vmem_rows = buf.at[slot, :, b, :]
        src, dst = (vmem_rows, hbm_rows) if to_hbm else (hbm_rows, vmem_rows)
        copies.append(pltpu.make_async_copy(src, dst, sems.at[slot, b]))
    return copies


def _fetch_time_major(x_hbm, buf, sems, n_blocks):
    i = pl.program_id(0)

    @pl.when(i == 0)
    def _():
        for c in _batch_row_copies(x_hbm, buf, sems, 0, 0, to_hbm=False):
            c.start()

    @pl.when(i + 1 < n_blocks)
    def _():
        for c in _batch_row_copies(x_hbm, buf, sems, i + 1, (i + 1) % 2, to_hbm=False):
            c.start()

    for c in _batch_row_copies(x_hbm, buf, sems, i, i % 2, to_hbm=False):
        c.wait()
    return buf[i % 2].reshape(buf.shape[1] * BATCH, D_MODEL)


def _store_batch_major(value, o_hbm, buf, sems, n_blocks):
    i = pl.program_id(0)
    slot = i % 2

    @pl.when(i >= 2)
    def _():
        for c in _batch_row_copies(o_hbm, buf, sems, i - 2, slot, to_hbm=True):
            c.wait()

    buf[slot] = value.reshape(buf.shape[1], BATCH, D_MODEL)
    for c in _batch_row_copies(o_hbm, buf, sems, i, slot, to_hbm=True):
        c.start()

    @pl.when(i == n_blocks - 1)
    def _():
        for back in range(min(2, n_blocks)):
            for c in _batch_row_copies(o_hbm, buf, sems, i - back, (i - back) % 2, to_hbm=True):
                c.wait()


def _ada_kernel(c_ref, w_ref, b_ref, o_ref):
    c_act = jax.nn.silu(c_ref[...])
    o_ref[0] = jnp.dot(c_act, w_ref[0], preferred_element_type=jnp.float32) + b_ref[0]


def _ada_mods(c, w_ada, b_ada):
    n_cols = 6 * D_MODEL
    return pl.pallas_call(
        _ada_kernel,
        grid=(DEPTH, n_cols // ADA_COLS),
        in_specs=[
            pl.BlockSpec((BATCH, D_MODEL), lambda i, j: (0, 0)),
            pl.BlockSpec((1, D_MODEL, ADA_COLS), lambda i, j: (i, 0, j)),
            pl.BlockSpec((1, 1, ADA_COLS), lambda i, j: (i, 0, j)),
        ],
        out_specs=pl.BlockSpec((1, BATCH, ADA_COLS), lambda i, j: (i, 0, j)),
        out_shape=jax.ShapeDtypeStruct((DEPTH, BATCH, n_cols), jnp.float32),
        compiler_params=pltpu.CompilerParams(
            dimension_semantics=("arbitrary", "arbitrary"),
            vmem_limit_bytes=VMEM_LIMIT_BYTES),
        name="ada_mods",
    )(c, w_ada, b_ada.reshape(DEPTH, 1, n_cols))


def _ssm_param_kernel(are_ref, aim_ref, ls_ref, btr_ref, bti_ref, ctr_ref, cti_ref,
                      wb_ref, wc_ref, a_ref):
    bf16 = jnp.bfloat16
    lr = jnp.minimum(are_ref[0], -1e-4)
    li = aim_ref[0]
    dt = jnp.exp(ls_ref[0])
    mag = jnp.exp(lr * dt)
    abr = mag * jnp.cos(li * dt)
    abi = mag * jnp.sin(li * dt)
    den = lr * lr + li * li
    qr = ((abr - 1.0) * lr + abi * li) / den
    qi = (abi * lr - (abr - 1.0) * li) / den
    a_ref[0, :, 0] = jnp.broadcast_to(abr[:, None, :], (N_CHUNKS, SUBLANES, LANES))
    a_ref[0, :, 1] = jnp.broadcast_to(abi[:, None, :], (N_CHUNKS, SUBLANES, LANES))

    btr = btr_ref[0]
    bti = bti_ref[0]
    bbr = qr[:, None, :] * btr - qi[:, None, :] * bti
    bbi = qr[:, None, :] * bti + qi[:, None, :] * btr

    b_row_group = lax.broadcasted_iota(jnp.int32, (MXU_DIM, LANES), 0) // SSM_GROUP
    b_lane_sub = lax.broadcasted_iota(jnp.int32, (MXU_DIM, LANES), 1) // SSM_STATE
    c_col_group = lax.broadcasted_iota(jnp.int32, (LANES, MXU_DIM), 1) // SSM_GROUP
    c_row_sub = lax.broadcasted_iota(jnp.int32, (LANES, MXU_DIM), 0) // SSM_STATE
    spread = (lax.broadcasted_iota(jnp.int32, (SSM_GROUP, MXU_DIM), 1) % SSM_GROUP
              == lax.broadcasted_iota(jnp.int32, (SSM_GROUP, MXU_DIM), 0)).astype(bf16)

    def b_tile(block, live):
        rep = jnp.broadcast_to(block[None], (GROUPS_PER_SLAB, SSM_GROUP, LANES))
        return jnp.where(live, rep.reshape(MXU_DIM, LANES), 0.0).astype(bf16)

    def c_tile(block, live):
        rep = jnp.dot(block.astype(bf16), spread, preferred_element_type=jnp.float32)
        return jnp.where(live, rep, 0.0).astype(bf16)

    for j in range(N_CHUNKS):
        slab, k = divmod(j, CHUNKS_PER_SLAB)
        b_live = b_row_group == k * GROUPS_PER_CHUNK + b_lane_sub
        wb_ref[0, j, :, pl.ds(0, LANES)] = b_tile(bbr[j], b_live)
        wb_ref[0, j, :, pl.ds(LANES, LANES)] = b_tile(bbi[j], b_live)
        c_live = c_col_group == k * GROUPS_PER_CHUNK + c_row_sub
        row = k * 2 * LANES
        wc_ref[0, slab, pl.ds(row, LANES), :] = c_tile(ctr_ref[0, j], c_live)
        wc_ref[0, slab, pl.ds(row + LANES, LANES), :] = c_tile(-cti_ref[0, j], c_live)


def _ssm_prep(a_re, a_im, log_step, b_re, b_im, c_re, c_im):
    n = a_re.shape[0]
    f32 = jnp.float32
    g, p, h = SSM_GROUPS, SSM_STATE, SSM_GROUP

    def chunk_lanes(v):
        return v.reshape(n, N_CHUNKS, LANES)

    def b_blocks(v):
        v = v.reshape(n, N_CHUNKS, GROUPS_PER_CHUNK, p, h)
        return jnp.transpose(v, (0, 1, 4, 2, 3)).reshape(n, N_CHUNKS, h, LANES)

    def c_blocks(v):
        v = v.reshape(n, N_CHUNKS, GROUPS_PER_CHUNK, h, p)
        return jnp.transpose(v, (0, 1, 2, 4, 3)).reshape(n, N_CHUNKS, LANES, h)

    ls = chunk_lanes(jnp.broadcast_to(log_step[:, :, None], (n, g, p)))
    lane_spec = pl.BlockSpec((1, N_CHUNKS, LANES), lambda i: (i, 0, 0))
    b_spec = pl.BlockSpec((1, N_CHUNKS, h, LANES), lambda i: (i, 0, 0, 0))
    c_spec = pl.BlockSpec((1, N_CHUNKS, LANES, h), lambda i: (i, 0, 0, 0))
    return pl.pallas_call(
        _ssm_param_kernel,
        grid=(n,),
        in_specs=[lane_spec, lane_spec, lane_spec, b_spec, b_spec, c_spec, c_spec],
        out_specs=(
            pl.BlockSpec((1, N_CHUNKS, MXU_DIM, 2 * LANES), lambda i: (i, 0, 0, 0)),
            pl.BlockSpec((1, N_SLABS, SLAB_STATE, MXU_DIM), lambda i: (i, 0, 0, 0)),
            pl.BlockSpec((1, N_CHUNKS, 2, SUBLANES, LANES), lambda i: (i, 0, 0, 0, 0)),
        ),
        out_shape=(
            jax.ShapeDtypeStruct((n, N_CHUNKS, MXU_DIM, 2 * LANES), jnp.bfloat16),
            jax.ShapeDtypeStruct((n, N_SLABS, SLAB_STATE, MXU_DIM), jnp.bfloat16),
            jax.ShapeDtypeStruct((n, N_CHUNKS, 2, SUBLANES, LANES), f32),
        ),
        compiler_params=pltpu.CompilerParams(
            dimension_semantics=("arbitrary",), vmem_limit_bytes=VMEM_LIMIT_BYTES),
        name="ssm_params",
    )(chunk_lanes(a_re), chunk_lanes(a_im), ls, b_blocks(b_re), b_blocks(b_im),
      c_blocks(c_re), c_blocks(c_im))


def _s5_kernel(x_ref, mods_ref, gain_ref, wb_ref, wc_ref, a_ref, d_ref, wout_hbm,
               o_ref, bu_ref, xs_ref, state_ref, wout_ref, wout_stage, wout_sems, *fetch_scratch,
               steps, n_blocks, w_layer):
    @pl.when(pl.program_id(0) == 0)
    def _():
        state_ref[...] = jnp.zeros_like(state_ref)
        _stage_bf16(wout_hbm, w_layer, wout_ref, wout_stage, wout_sems)

    if fetch_scratch:
        x = _fetch_time_major(x_ref, *fetch_scratch, n_blocks)
    else:
        x = x_ref[...]
    mods = mods_ref[0]
    shift, scale, gate = (mods[:, k * D_MODEL:(k + 1) * D_MODEL] for k in range(3))
    h = _norm_modulate(x, gain_ref[0], shift, scale)
    hb = h.astype(jnp.bfloat16)

    half_rows = x.shape[0] // 2
    half_steps = steps // 2
    carry = [(state_ref[j, 0], state_ref[j, 1]) for j in range(N_CHUNKS)]
    for half in range(2):
        r0 = half * half_rows
        ys = []
        for slab in range(N_SLABS):
            chunks = range(slab * CHUNKS_PER_SLAB, (slab + 1) * CHUNKS_PER_SLAB)
            u = hb[r0:r0 + half_rows, slab * MXU_DIM:(slab + 1) * MXU_DIM]
            for j in chunks:
                bu_ref[j, pl.ds(r0, half_rows), :] = jnp.dot(
                    u, wb_ref[0, j], preferred_element_type=jnp.float32)

            for t0 in range(half * half_steps, (half + 1) * half_steps, BF16_ROWS // SUBLANES):
                for k, j in enumerate(chunks):
                    xr, xi = carry[j]
                    a_re, a_im = a_ref[0, j, 0], a_ref[0, j, 1]
                    outs_r, outs_i = [], []
                    for t in range(t0, t0 + BF16_ROWS // SUBLANES):
                        rows = pl.ds(t * SUBLANES, SUBLANES)
                        bu_r = bu_ref[j, rows, pl.ds(0, LANES)]
                        bu_i = bu_ref[j, rows, pl.ds(LANES, LANES)]
                        xr, xi = (a_re * xr - a_im * xi + bu_r, a_re * xi + a_im * xr + bu_i)
                        outs_r.append(xr)
                        outs_i.append(xi)
                    carry[j] = (xr, xi)
                    rows = pl.ds(t0 * SUBLANES, BF16_ROWS)
                    col = k * 2 * LANES
                    xs_ref[slab, rows, pl.ds(col, LANES)] = (
                        jnp.concatenate(outs_r, axis=0).astype(jnp.bfloat16))
                    xs_ref[slab, rows, pl.ds(col + LANES, LANES)] = (
                        jnp.concatenate(outs_i, axis=0).astype(jnp.bfloat16))

            ys.append(jnp.dot(xs_ref[slab, pl.ds(r0, half_rows), :], wc_ref[0, slab],
                              preferred_element_type=jnp.float32))

        y = jnp.concatenate(ys, axis=-1) + d_ref[0] * h[r0:r0 + half_rows]
        y = jax.nn.gelu(y).astype(jnp.bfloat16)
        z = jnp.dot(y, wout_ref[...], preferred_element_type=jnp.float32)
        mix = z[:, :D_MODEL] * jax.nn.sigmoid(z[:, D_MODEL:])
        o_ref[pl.ds(r0, half_rows), :] = _gated_residual(x[r0:r0 + half_rows], gate, mix)

    for j in range(N_CHUNKS):
        state_ref[j, 0] = carry[j][0]
        state_ref[j, 1] = carry[j][1]


def _s5_layer(x_in, mods, layer, j, gain, wb, wc, a, d, w_out):
    steps = ROWS_MIX // BATCH
    batch_major = x_in.ndim == 3
    rows = x_in.shape[0] * x_in.shape[1] if batch_major else x_in.shape[0]
    n_blocks = rows // ROWS_MIX
    if batch_major:
        x_spec = pl.BlockSpec(memory_space=pl.ANY)
        fetch_scratch = [pltpu.VMEM((2, steps, BATCH, D_MODEL), jnp.float32),
                         pltpu.SemaphoreType.DMA((2, BATCH))]
    else:
        x_spec = pl.BlockSpec((ROWS_MIX, D_MODEL), lambda i: (i, 0))
        fetch_scratch = []
    return pl.pallas_call(
        functools.partial(_s5_kernel, steps=steps, n_blocks=n_blocks, w_layer=j),
        grid=(n_blocks,),
        in_specs=[
            x_spec,
            pl.BlockSpec((1, BATCH, 6 * D_MODEL), lambda i: (layer, 0, 0)),
            pl.BlockSpec((1, 1, D_MODEL), lambda i: (layer, 0, 0)),
            _layer_spec(wb.shape, j),
            _layer_spec(wc.shape, j),
            _layer_spec(a.shape, j),
            _layer_spec(d.shape, j),
            pl.BlockSpec(memory_space=pl.ANY),
        ],
        out_specs=pl.BlockSpec((ROWS_MIX, D_MODEL), lambda i: (i, 0)),
        out_shape=jax.ShapeDtypeStruct((rows, D_MODEL), x_in.dtype),
        scratch_shapes=[
            pltpu.VMEM((N_CHUNKS, ROWS_MIX, 2 * LANES), jnp.float32),
            pltpu.VMEM((N_SLABS, ROWS_MIX, SLAB_STATE), jnp.bfloat16),
            pltpu.VMEM((N_CHUNKS, 2, SUBLANES, LANES), jnp.float32),
        ] + _stage_scratch(w_out.shape) + fetch_scratch,
        compiler_params=pltpu.CompilerParams(
            dimension_semantics=("arbitrary",), vmem_limit_bytes=VMEM_LIMIT_BYTES),
        name="s5_layer",
    )(x_in, mods, gain, wb, wc, a, d, w_out)


def _conv_kernel(x_ref, mods_ref, gain_ref, win_hbm, cw_ref, wout_hbm, o_ref, cv_ref,
                 win_ref, win_stage, win_sems, wout_ref, wout_stage, wout_sems, *, w_layer):
    rows = x_ref.shape[0]
    hist = (CONV_WIDTH - 1) * BATCH

    @pl.when(pl.program_id(0) == 0)
    def _():
        cv_ref[pl.ds(0, hist), :] = jnp.zeros((hist, D_MODEL), jnp.float32)
        _stage_bf16(win_hbm, w_layer, win_ref, win_stage, win_sems)
        _stage_bf16(wout_hbm, w_layer, wout_ref, wout_stage, wout_sems)

    x = x_ref[...]
    mods = mods_ref[0]
    shift, scale, gate = (mods[:, k * D_MODEL:(k + 1) * D_MODEL] for k in range(3))
    h = _norm_modulate(x, gain_ref[0], shift, scale)
    proj = jnp.dot(h.astype(jnp.bfloat16), win_ref[...], preferred_element_type=jnp.float32)
    bg = proj[:, :D_MODEL]
    cv = proj[:, D_MODEL:2 * D_MODEL] * proj[:, 2 * D_MODEL:]
    cv_ref[pl.ds(hist, rows), :] = cv
    cw = cw_ref[0]
    conv = (cw[2:3] * cv + cw[1:2] * cv_ref[pl.ds(BATCH, rows), :]
            + cw[0:1] * cv_ref[pl.ds(0, rows), :])
    cv_ref[pl.ds(0, hist), :] = cv_ref[pl.ds(rows, hist), :]
    out = jnp.dot((bg * conv).astype(jnp.bfloat16), wout_ref[...],
                  preferred_element_type=jnp.float32)
    o_ref[...] = _gated_residual(x, gate, out)


def _conv_layer(x2, mods, layer, j, gain, w_in, conv_w, w_out):
    rows = x2.shape[0]
    hist = (CONV_WIDTH - 1) * BATCH
    return pl.pallas_call(
        functools.partial(_conv_kernel, w_layer=j),
        grid=(rows // ROWS_MIX,),
        in_specs=[
            pl.BlockSpec((ROWS_MIX, D_MODEL), lambda i: (i, 0)),
            pl.BlockSpec((1, BATCH, 6 * D_MODEL), lambda i: (layer, 0, 0)),
            pl.BlockSpec((1, 1, D_MODEL), lambda i: (layer, 0, 0)),
            pl.BlockSpec(memory_space=pl.ANY),
            _layer_spec(conv_w.shape, j),
            pl.BlockSpec(memory_space=pl.ANY),
        ],
        out_specs=pl.BlockSpec((ROWS_MIX, D_MODEL), lambda i: (i, 0)),
        out_shape=jax.ShapeDtypeStruct(x2.shape, x2.dtype),
        scratch_shapes=([pltpu.VMEM((ROWS_MIX + hist, D_MODEL), jnp.float32)]
                        + _stage_scratch(w_in.shape) + _stage_scratch(w_out.shape)),
        compiler_params=pltpu.CompilerParams(
            dimension_semantics=("arbitrary",), vmem_limit_bytes=VMEM_LIMIT_BYTES),
        name="conv_layer",
    )(x2, mods, gain, w_in, conv_w, w_out)


def _ffn_kernel(x_ref, mods_ref, gain_ref, win_hbm, wout_hbm, fg_ref, o_ref,
                win_ref, win_stage, win_sems, wout_ref, wout_stage, wout_sems, *store_scratch,
                final_norm, n_blocks, w_layer):
    hidden = wout_ref.shape[0]

    @pl.when(pl.program_id(0) == 0)
    def _():
        _stage_bf16(win_hbm, w_layer, win_ref, win_stage, win_sems)
        _stage_bf16(wout_hbm, w_layer, wout_ref, wout_stage, wout_sems)

    x = x_ref[...]
    mods = mods_ref[0]
    shift, scale, gate = (mods[:, k * D_MODEL:(k + 1) * D_MODEL] for k in range(3, 6))
    h = _norm_modulate(x, gain_ref[0], shift, scale)
    gu = jnp.dot(h.astype(jnp.bfloat16), win_ref[...], preferred_element_type=jnp.float32)
    act = (jax.nn.silu(gu[:, :hidden]) * gu[:, hidden:]).astype(jnp.bfloat16)
    out = jnp.dot(act, wout_ref[...], preferred_element_type=jnp.float32)
    xn = _gated_residual(x, gate, out)
    if final_norm:
        xn = xn * lax.rsqrt(jnp.mean(xn * xn, axis=-1, keepdims=True) + RMS_EPS) * fg_ref[...]
    if store_scratch:
        _store_batch_major(xn, o_ref, *store_scratch, n_blocks)
    else:
        o_ref[...] = xn


def _ffn_layer(x2, mods, layer, gain, w_in, w_out, final_g, final_norm, batch_major_out):
    rows = x2.shape[0]
    n_blocks = rows // ROWS_FFN
    if batch_major_out:
        steps = ROWS_FFN // BATCH
        out_spec = pl.BlockSpec(memory_space=pl.ANY)
        out_shape = jax.ShapeDtypeStruct((BATCH, rows // BATCH, D_MODEL), x2.dtype)
        store_scratch = [pltpu.VMEM((2, steps, BATCH, D_MODEL), jnp.float32),
                         pltpu.SemaphoreType.DMA((2, BATCH))]
    else:
        out_spec = pl.BlockSpec((ROWS_FFN, D_MODEL), lambda i: (i, 0))
        out_shape = jax.ShapeDtypeStruct(x2.shape, x2.dtype)
        store_scratch = []
    return pl.pallas_call(
        functools.partial(_ffn_kernel, final_norm=final_norm, n_blocks=n_blocks, w_layer=layer),
        grid=(n_blocks,),
        in_specs=[
            pl.BlockSpec((ROWS_FFN, D_MODEL), lambda i: (i, 0)),
            pl.BlockSpec((1, BATCH, 6 * D_MODEL), lambda i: (layer, 0, 0)),
            pl.BlockSpec((1, 1, D_MODEL), lambda i: (layer, 0, 0)),
            pl.BlockSpec(memory_space=pl.ANY),
            pl.BlockSpec(memory_space=pl.ANY),
            _const_spec((1, D_MODEL)),
        ],
        out_specs=out_spec,
        out_shape=out_shape,
        scratch_shapes=_stage_scratch(w_in.shape) + _stage_scratch(w_out.shape) + store_scratch,
        compiler_params=pltpu.CompilerParams(
            dimension_semantics=("arbitrary",), vmem_limit_bytes=VMEM_LIMIT_BYTES),
        name="ffn_layer",
    )(x2, mods, gain, w_in, w_out, final_g.reshape(1, D_MODEL))


def kernel(x, c, norm1_g, norm2_g, w_ada, b_ada, ssm_a_re, ssm_a_im, ssm_log_step, ssm_b_re, ssm_b_im, ssm_c_re, ssm_c_im, ssm_d, ssm_w_out, conv_w_in, conv_w, conv_w_out, w_ffn_in, w_ffn_out, final_g):
    mods = _ada_mods(c, w_ada, b_ada)
    gain1 = norm1_g.reshape(DEPTH, 1, D_MODEL)
    gain2 = norm2_g.reshape(DEPTH, 1, D_MODEL)
    wb, wc, a = _ssm_prep(ssm_a_re, ssm_a_im, ssm_log_step, ssm_b_re, ssm_b_im,
                          ssm_c_re, ssm_c_im)
    ssm_d3 = ssm_d.reshape(-1, 1, D_MODEL)
    x2 = x
    for i in range(DEPTH):
        j = i // N_MIXERS
        if i % N_MIXERS == 0:
            x2 = _s5_layer(x2, mods, i, j, gain1, wb, wc, a, ssm_d3, ssm_w_out)
        else:
            x2 = _conv_layer(x2, mods, i, j, gain1, conv_w_in, conv_w, conv_w_out)
        last = i == DEPTH - 1
        x2 = _ffn_layer(x2, mods, i, gain2, w_ffn_in, w_ffn_out, final_g,
                        final_norm=last, batch_major_out=last)
    return x2
```

```python
import functools

import jax
import jax.numpy as jnp
from jax import lax
from jax.experimental import pallas as pl
from jax.experimental.pallas import tpu as pltpu

D_MODEL = 1024
BATCH = 8
DEPTH = 4
N_MIXERS = 2
SSM_GROUP = 16
SSM_GROUPS = D_MODEL // SSM_GROUP
SSM_STATE = 64
CONV_WIDTH = 3
RMS_EPS = 1e-6

LANES = 128
SUBLANES = 8
BF16_ROWS = 16
VMEM_LIMIT_BYTES = 56 * 1024 * 1024
STAGE_BYTES = 3 * 512 * 1024

N_CHUNKS = SSM_GROUPS * SSM_STATE // LANES
GROUPS_PER_CHUNK = LANES // SSM_STATE
N_COLS = D_MODEL // LANES
GROUPS_PER_COL = LANES // SSM_GROUP
CHUNKS_PER_COL = N_CHUNKS // N_COLS
COL_K = CHUNKS_PER_COL * 2 * LANES + LANES

ROWS_FFN = 512
ROWS_MIX = 512
ADA_COLS = 1536


def _const_spec(shape, index=None):
    index = (0,) * len(shape) if index is None else index
    return pl.BlockSpec(shape, lambda *_: index, pipeline_mode=pl.Buffered(1))


def _layer_spec(shape, layer):
    return _const_spec((1,) + tuple(shape[1:]), (layer,) + (0,) * (len(shape) - 1))


def _norm_modulate(x, gain, shift, scale):
    rows = x.shape[0]
    y = x * lax.rsqrt(jnp.mean(x * x, axis=-1, keepdims=True) + RMS_EPS)
    y = y.reshape(rows // BATCH, BATCH, D_MODEL)
    return (y * (gain * (1.0 + scale))[None] + shift[None]).reshape(rows, D_MODEL)


def _gated_residual(x, gate, update):
    rows = x.shape[0]
    upd = update.reshape(rows // BATCH, BATCH, D_MODEL) * gate[None]
    return x + upd.reshape(rows, D_MODEL)


def _stage_scratch(shape):
    k_dim, n_dim = shape[-2:]
    fits = [r for r in range(BF16_ROWS, k_dim + 1, BF16_ROWS)
            if k_dim % r == 0 and r * n_dim * 4 <= STAGE_BYTES]
    return [pltpu.VMEM((k_dim, n_dim), jnp.bfloat16),
            pltpu.VMEM((2, max(fits), n_dim), jnp.float32),
            pltpu.SemaphoreType.DMA((2,))]


def _stage_bf16(w_hbm, layer, w_bf16, stage, sems):
    chunk = stage.shape[1]
    n_chunks = w_bf16.shape[0] // chunk

    def copy(k):
        return pltpu.make_async_copy(w_hbm.at[layer, pl.ds(k * chunk, chunk), :],
                                     stage.at[k % 2], sems.at[k % 2])

    copy(0).start()
    for k in range(n_chunks):
        if k + 1 < n_chunks:
            copy(k + 1).start()
        copy(k).wait()
        w_bf16[pl.ds(k * chunk, chunk), :] = stage[k % 2].astype(jnp.bfloat16)


def _batch_row_copies(hbm, buf, sems, block, slot, to_hbm):
    steps = buf.shape[1]
    copies = []
    for b in range(BATCH):
        hbm_rows = hbm.at[b, pl.ds(block * steps, steps), :]
        vmem_rows = buf.at[slot, :, b, :]
        src, dst = (vmem_rows, hbm_rows) if to_hbm else (hbm_rows, vmem_rows)
        copies.append(pltpu.make_async_copy(src, dst, sems.at[slot, b]))
    return copies


def _fetch_time_major(x_hbm, buf, sems, n_blocks):
    i = pl.program_id(0)

    @pl.when(i == 0)
    def _():
        for c in _batch_row_copies(x_hbm, buf, sems, 0, 0, to_hbm=False):
            c.start()

    @pl.when(i + 1 < n_blocks)
    def _():
        for c in _batch_row_copies(x_hbm, buf, sems, i + 1, (i + 1) % 2, to_hbm=False):
            c.start()

    for c in _batch_row_copies(x_hbm, buf, sems, i, i % 2, to_hbm=False):
        c.wait()
    return buf[i % 2].reshape(buf.shape[1] * BATCH, D_MODEL)


def _store_batch_major(value, o_hbm, buf, sems, n_blocks):
    i = pl.program_id(0)
    slot = i % 2

    @pl.when(i >= 2)
    def _():
        for c in _batch_row_copies(o_hbm, buf, sems, i - 2, slot, to_hbm=True):
            c.wait()

    buf[slot] = value.reshape(buf.shape[1], BATCH, D_MODEL)
    for c in _batch_row_copies(o_hbm, buf, sems, i, slot, to_hbm=True):
        c.start()

    @pl.when(i == n_blocks - 1)
    def _():
        for back in range(min(2, n_blocks)):
            for c in _batch_row_copies(o_hbm, buf, sems, i - back, (i - back) % 2, to_hbm=True):
                c.wait()


def _ada_kernel(c_ref, w_ref, b_ref, o_ref):
    c_act = jax.nn.silu(c_ref[...])
    o_ref[0] = jnp.dot(c_act, w_ref[0], preferred_element_type=jnp.float32) + b_ref[0]


def _ada_mods(c, w_ada, b_ada):
    n_cols = 6 * D_MODEL
    return pl.pallas_call(
        _ada_kernel,
        grid=(DEPTH, n_cols // ADA_COLS),
        in_specs=[
            pl.BlockSpec((BATCH, D_MODEL), lambda i, j: (0, 0)),
            pl.BlockSpec((1, D_MODEL, ADA_COLS), lambda i, j: (i, 0, j)),
            pl.BlockSpec((1, 1, ADA_COLS), lambda i, j: (i, 0, j)),
        ],
        out_specs=pl.BlockSpec((1, BATCH, ADA_COLS), lambda i, j: (i, 0, j)),
        out_shape=jax.ShapeDtypeStruct((DEPTH, BATCH, n_cols), jnp.float32),
        compiler_params=pltpu.CompilerParams(
            dimension_semantics=("arbitrary", "arbitrary"),
            vmem_limit_bytes=VMEM_LIMIT_BYTES),
        name="ada_mods",
    )(c, w_ada, b_ada.reshape(DEPTH, 1, n_cols))


def _zoh(a_re, a_im, log_step):
    lr = jnp.minimum(a_re, -1e-4)
    li = a_im
    dt = jnp.exp(log_step)
    mag = jnp.exp(lr * dt)
    abr = mag * jnp.cos(li * dt)
    abi = mag * jnp.sin(li * dt)
    den = lr * lr + li * li
    qr = ((abr - 1.0) * lr + abi * li) / den
    qi = (abi * lr - (abr - 1.0) * li) / den
    return abr, abi, qr, qi


def _ssm_param_kernel(are_ref, aim_ref, ls_ref, btr_ref, bti_ref,
                      aret_ref, aimt_ref, lst_ref, ctr_ref, cti_ref,
                      wz_ref, wy_ref, a2_ref):
    bf16 = jnp.bfloat16
    abr, abi, qr, qi = _zoh(are_ref[0], aim_ref[0], ls_ref[0])
    a2r = abr * abr - abi * abi
    a2i = 2.0 * abr * abi
    a2_ref[0, :, 0] = jnp.broadcast_to(a2r[:, None, :], (N_CHUNKS, SUBLANES, LANES))
    a2_ref[0, :, 1] = jnp.broadcast_to(a2i[:, None, :], (N_CHUNKS, SUBLANES, LANES))
    btr = btr_ref[0]
    bti = bti_ref[0]
    bbr = qr[:, None, :] * btr - qi[:, None, :] * bti
    bbi = qr[:, None, :] * bti + qi[:, None, :] * btr
    bar = abr[:, None, :] * bbr - abi[:, None, :] * bbi
    bai = abr[:, None, :] * bbi + abi[:, None, :] * bbr
    abr_t, abi_t, _, _ = _zoh(aret_ref[0], aimt_ref[0], lst_ref[0])
    ctr = ctr_ref[0]
    cti = cti_ref[0]

    row_group = lax.broadcasted_iota(jnp.int32, (LANES, LANES), 0) // SSM_GROUP
    col_group = lax.broadcasted_iota(jnp.int32, (LANES, LANES), 1) // SSM_GROUP
    row_sub = lax.broadcasted_iota(jnp.int32, (LANES, LANES), 0) // SSM_STATE
    col_sub = lax.broadcasted_iota(jnp.int32, (LANES, LANES), 1) // SSM_STATE
    lane_sub = lax.broadcasted_iota(jnp.int32, (SSM_GROUP, LANES), 1) // SSM_STATE
    spread = (lax.broadcasted_iota(jnp.int32, (SSM_GROUP, LANES), 1) % SSM_GROUP
              == lax.broadcasted_iota(jnp.int32, (SSM_GROUP, LANES), 0)).astype(bf16)

    def rows_by_group(block):
        rep = jnp.broadcast_to(block[None], (GROUPS_PER_COL, SSM_GROUP, block.shape[-1]))
        return rep.reshape(LANES, block.shape[-1])

    def cols_by_group(block):
        return jnp.dot(block.astype(bf16), spread, preferred_element_type=jnp.float32)

    for j in range(N_CHUNKS):
        v, c = divmod(j, CHUNKS_PER_COL)
        b_live = row_group == c * GROUPS_PER_CHUNK + col_sub
        for half, (re, im) in enumerate(((bar, bai), (bbr, bbi))):
            rows = pl.ds(half * LANES, LANES)
            wz_ref[0, j, rows, pl.ds(0, LANES)] = (
                jnp.where(b_live, rows_by_group(re[j]), 0.0).astype(bf16))
            wz_ref[0, j, rows, pl.ds(LANES, LANES)] = (
                jnp.where(b_live, rows_by_group(im[j]), 0.0).astype(bf16))
        c_live = col_group == c * GROUPS_PER_CHUNK + row_sub
        a_r, a_i = abr_t[:, j:j + 1], abi_t[:, j:j + 1]
        car = ctr[j] * a_r - cti[j] * a_i
        cai = ctr[j] * a_i + cti[j] * a_r
        base = c * 2 * LANES
        for part, (even, odd) in enumerate(((ctr[j], car), (-cti[j], -cai))):
            rows = pl.ds(base + part * LANES, LANES)
            wy_ref[0, v, rows, pl.ds(0, LANES)] = (
                jnp.where(c_live, cols_by_group(even), 0.0).astype(bf16))
            wy_ref[0, v, rows, pl.ds(LANES, LANES)] = (
                jnp.where(c_live, cols_by_group(odd), 0.0).astype(bf16))

    for v in range(N_COLS):
        acc = jnp.zeros((LANES, LANES), jnp.float32)
        for c in range(CHUNKS_PER_COL):
            j = v * CHUNKS_PER_COL + c
            for l in range(GROUPS_PER_CHUNK):
                mine = lane_sub == l
                cb = (jnp.dot(jnp.where(mine, bbr[j], 0.0), ctr[j],
                              preferred_element_type=jnp.float32)
                      - jnp.dot(jnp.where(mine, bbi[j], 0.0), cti[j],
                                preferred_element_type=jnp.float32))
                group = c * GROUPS_PER_CHUNK + l
                on_diag = (row_group == group) & (col_group == group)
                acc = acc + jnp.where(on_diag, cols_by_group(rows_by_group(cb)), 0.0)
        rows = pl.ds(CHUNKS_PER_COL * 2 * LANES, LANES)
        wy_ref[0, v, rows, pl.ds(0, LANES)] = jnp.zeros((LANES, LANES), bf16)
        wy_ref[0, v, rows, pl.ds(LANES, LANES)] = acc.astype(bf16)


def _ssm_prep(a_re, a_im, log_step, b_re, b_im, c_re, c_im):
    n = a_re.shape[0]
    f32 = jnp.float32
    g, p, h = SSM_GROUPS, SSM_STATE, SSM_GROUP

    def chunk_lanes(v):
        return v.reshape(n, N_CHUNKS, LANES)

    def chunk_rows(v):
        return jnp.swapaxes(chunk_lanes(v), 1, 2)

    def b_blocks(v):
        v = v.reshape(n, N_CHUNKS, GROUPS_PER_CHUNK, p, h)
        return jnp.transpose(v, (0, 1, 4, 2, 3)).reshape(n, N_CHUNKS, h, LANES)

    def c_blocks(v):
        v = v.reshape(n, N_CHUNKS, GROUPS_PER_CHUNK, h, p)
        return jnp.transpose(v, (0, 1, 2, 4, 3)).reshape(n, N_CHUNKS, LANES, h)

    ls = jnp.broadcast_to(log_step[:, :, None], (n, g, p))
    lane_spec = pl.BlockSpec((1, N_CHUNKS, LANES), lambda i: (i, 0, 0))
    b_spec = pl.BlockSpec((1, N_CHUNKS, h, LANES), lambda i: (i, 0, 0, 0))
    row_spec = pl.BlockSpec((1, LANES, N_CHUNKS), lambda i: (i, 0, 0))
    c_spec = pl.BlockSpec((1, N_CHUNKS, LANES, h), lambda i: (i, 0, 0, 0))
    return pl.pallas_call(
        _ssm_param_kernel,
        grid=(n,),
        in_specs=[lane_spec, lane_spec, lane_spec, b_spec, b_spec,
                  row_spec, row_spec, row_spec, c_spec, c_spec],
        out_specs=(
            pl.BlockSpec((1, N_CHUNKS, 2 * LANES, 2 * LANES), lambda i: (i, 0, 0, 0)),
            pl.BlockSpec((1, N_COLS, COL_K, 2 * LANES), lambda i: (i, 0, 0, 0)),
            pl.BlockSpec((1, N_CHUNKS, 2, SUBLANES, LANES), lambda i: (i, 0, 0, 0, 0)),
        ),
        out_shape=(
            jax.ShapeDtypeStruct((n, N_CHUNKS, 2 * LANES, 2 * LANES), jnp.bfloat16),
            jax.ShapeDtypeStruct((n, N_COLS, COL_K, 2 * LANES), jnp.bfloat16),
            jax.ShapeDtypeStruct((n, N_CHUNKS, 2, SUBLANES, LANES), f32),
        ),
        compiler_params=pltpu.CompilerParams(
            dimension_semantics=("arbitrary",), vmem_limit_bytes=VMEM_LIMIT_BYTES),
        name="ssm_params",
    )(chunk_lanes(a_re), chunk_lanes(a_im), chunk_lanes(ls), b_blocks(b_re), b_blocks(b_im),
      chunk_rows(a_re), chunk_rows(a_im), chunk_rows(ls), c_blocks(c_re), c_blocks(c_im))


def _s5_kernel(x_ref, mods_ref, gain_ref, wz_ref, wy_ref, a2_ref, d_ref, wout_hbm,
               o_ref, z_ref, xs_ref, state_ref, oprev_ref, wout_ref, wout_stage, wout_sems,
               *fetch_scratch, pairs, n_blocks, w_layer):
    bf16 = jnp.bfloat16

    @pl.when(pl.program_id(0) == 0)
    def _():
        state_ref[...] = jnp.zeros_like(state_ref)
        oprev_ref[...] = jnp.zeros_like(oprev_ref)
        _stage_bf16(wout_hbm, w_layer, wout_ref, wout_stage, wout_sems)

    if fetch_scratch:
        x = _fetch_time_major(x_ref, *fetch_scratch, n_blocks)
    else:
        x = x_ref[...]
    mods = mods_ref[0]
    shift, scale, gate = (mods[:, k * D_MODEL:(k + 1) * D_MODEL] for k in range(3))

    half_rows = pairs * BATCH
    x_pairs = x.reshape(pairs, 2, BATCH, D_MODEL)
    x_even = x_pairs[:, 0].reshape(half_rows, D_MODEL)
    x_odd = x_pairs[:, 1].reshape(half_rows, D_MODEL)
    h_even = _norm_modulate(x_even, gain_ref[0], shift, scale)
    h_odd = _norm_modulate(x_odd, gain_ref[0], shift, scale)
    h_odd_prev = jnp.concatenate([oprev_ref[...], h_odd[:half_rows - BATCH]], axis=0)
    oprev_ref[...] = h_odd[half_rows - BATCH:]
    hb_even, hb_odd, hb_odd_prev = (t.astype(bf16) for t in (h_even, h_odd, h_odd_prev))

    ys_even, ys_odd = [], []
    for v in range(N_COLS):
        lanes = slice(v * LANES, (v + 1) * LANES)
        chunks = range(v * CHUNKS_PER_COL, (v + 1) * CHUNKS_PER_COL)
        lhs = jnp.concatenate([hb_odd_prev[:, lanes], hb_even[:, lanes]], axis=-1)
        for j in chunks:
            z_ref[j] = jnp.dot(lhs, wz_ref[0, j], preferred_element_type=jnp.float32)

        carry = [(state_ref[j, 0], state_ref[j, 1]) for j in chunks]
        for m0 in range(0, pairs, BF16_ROWS // SUBLANES):
            for c, j in enumerate(chunks):
                sr, si = carry[c]
                a_re, a_im = a2_ref[0, j, 0], a2_ref[0, j, 1]
                outs_r, outs_i = [], []
                for m in range(m0, m0 + BF16_ROWS // SUBLANES):
                    rows = pl.ds(m * SUBLANES, SUBLANES)
                    z_r = z_ref[j, rows, pl.ds(0, LANES)]
                    z_i = z_ref[j, rows, pl.ds(LANES, LANES)]
                    sr, si = (a_re * sr - a_im * si + z_r, a_re * si + a_im * sr + z_i)
                    outs_r.append(sr)
                    outs_i.append(si)
                carry[c] = (sr, si)
                rows = pl.ds(m0 * SUBLANES, BF16_ROWS)
                col = c * 2 * LANES
                xs_ref[v, rows, pl.ds(col, LANES)] = jnp.concatenate(outs_r, axis=0).astype(bf16)
                xs_ref[v, rows, pl.ds(col + LANES, LANES)] = (
                    jnp.concatenate(outs_i, axis=0).astype(bf16))
        for c, j in enumerate(chunks):
            state_ref[j, 0] = carry[c][0]
            state_ref[j, 1] = carry[c][1]

        lhs = jnp.concatenate([xs_ref[v], hb_odd[:, lanes]], axis=-1)
        y = jnp.dot(lhs, wy_ref[0, v], preferred_element_type=jnp.float32)
        ys_even.append(y[:, :LANES])
        ys_odd.append(y[:, LANES:])

    y = jnp.concatenate([jnp.concatenate(ys_even, axis=-1) + d_ref[0] * h_even,
                         jnp.concatenate(ys_odd, axis=-1) + d_ref[0] * h_odd], axis=0)
    y = jax.nn.gelu(y).astype(bf16)
    z = jnp.dot(y, wout_ref[...], preferred_element_type=jnp.float32)
    mix = z[:, :D_MODEL] * jax.nn.sigmoid(z[:, D_MODEL:])
    out = _gated_residual(jnp.concatenate([x_even, x_odd], axis=0), gate, mix)
    out_pairs = jnp.concatenate([out[:half_rows].reshape(pairs, 1, BATCH, D_MODEL),
                                 out[half_rows:].reshape(pairs, 1, BATCH, D_MODEL)], axis=1)
    o_ref[...] = out_pairs.reshape(2 * half_rows, D_MODEL)


def _s5_layer(x_in, mods, layer, j, gain, wz, wy, a2, d, w_out):
    steps = ROWS_MIX // BATCH
    pairs = steps // 2
    batch_major = x_in.ndim == 3
    rows = x_in.shape[0] * x_in.shape[1] if batch_major else x_in.shape[0]
    n_blocks = rows // ROWS_MIX
    if batch_major:
        x_spec = pl.BlockSpec(memory_space=pl.ANY)
        fetch_scratch = [pltpu.VMEM((2, steps, BATCH, D_MODEL), jnp.float32),
                         pltpu.SemaphoreType.DMA((2, BATCH))]
    else:
        x_spec = pl.BlockSpec((ROWS_MIX, D_MODEL), lambda i: (i, 0))
        fetch_scratch = []
    return pl.pallas_call(
        functools.partial(_s5_kernel, pairs=pairs, n_blocks=n_blocks, w_layer=j),
        grid=(n_blocks,),
        in_specs=[
            x_spec,
            pl.BlockSpec((1, BATCH, 6 * D_MODEL), lambda i: (layer, 0, 0)),
            pl.BlockSpec((1, 1, D_MODEL), lambda i: (layer, 0, 0)),
            _layer_spec(wz.shape, j),
            _layer_spec(wy.shape, j),
            _layer_spec(a2.shape, j),
            _layer_spec(d.shape, j),
            pl.BlockSpec(memory_space=pl.ANY),
        ],
        out_specs=pl.BlockSpec((ROWS_MIX, D_MODEL), lambda i: (i, 0)),
        out_shape=jax.ShapeDtypeStruct((rows, D_MODEL), x_in.dtype),
        scratch_shapes=[
            pltpu.VMEM((N_CHUNKS, pairs * BATCH, 2 * LANES), jnp.float32),
            pltpu.VMEM((N_COLS, pairs * BATCH, CHUNKS_PER_COL * 2 * LANES), jnp.bfloat16),
            pltpu.VMEM((N_CHUNKS, 2, SUBLANES, LANES), jnp.float32),
            pltpu.VMEM((BATCH, D_MODEL), jnp.float32),
        ] + _stage_scratch(w_out.shape) + fetch_scratch,
        compiler_params=pltpu.CompilerParams(
            dimension_semantics=("arbitrary",), vmem_limit_bytes=VMEM_LIMIT_BYTES),
        name="s5_layer",
    )(x_in, mods, gain, wz, wy, a2, d, w_out)


def _conv_kernel(x_ref, mods_ref, gain_ref, win_hbm, cw_ref, wout_hbm, o_ref, cv_ref,
                 win_ref, win_stage, win_sems, wout_ref, wout_stage, wout_sems, *, w_layer):
    rows = x_ref.shape[0]
    hist = (CONV_WIDTH - 1) * BATCH

    @pl.when(pl.program_id(0) == 0)
    def _():
        cv_ref[pl.ds(0, hist), :] = jnp.zeros((hist, D_MODEL), jnp.float32)
        _stage_bf16(win_hbm, w_layer, win_ref, win_stage, win_sems)
        _stage_bf16(wout_hbm, w_layer, wout_ref, wout_stage, wout_sems)

    x = x_ref[...]
    mods = mods_ref[0]
    shift, scale, gate = (mods[:, k * D_MODEL:(k + 1) * D_MODEL] for k in range(3))
    h = _norm_modulate(x, gain_ref[0], shift, scale)
    proj = jnp.dot(h.astype(jnp.bfloat16), win_ref[...], preferred_element_type=jnp.float32)
    bg = proj[:, :D_MODEL]
    cv = proj[:, D_MODEL:2 * D_MODEL] * proj[:, 2 * D_MODEL:]
    cv_ref[pl.ds(hist, rows), :] = cv
    cw = cw_ref[0]
    conv = (cw[2:3] * cv + cw[1:2] * cv_ref[pl.ds(BATCH, rows), :]
            + cw[0:1] * cv_ref[pl.ds(0, rows), :])
    cv_ref[pl.ds(0, hist), :] = cv_ref[pl.ds(rows, hist), :]
    out = jnp.dot((bg * conv).astype(jnp.bfloat16), wout_ref[...],
                  preferred_element_type=jnp.float32)
    o_ref[...] = _gated_residual(x, gate, out)


def _conv_layer(x2, mods, layer, j, gain, w_in, conv_w, w_out):
    rows = x2.shape[0]
    hist = (CONV_WIDTH - 1) * BATCH
    return pl.pallas_call(
        functools.partial(_conv_kernel, w_layer=j),
        grid=(rows // ROWS_MIX,),
        in_specs=[
            pl.BlockSpec((ROWS_MIX, D_MODEL), lambda i: (i, 0)),
            pl.BlockSpec((1, BATCH, 6 * D_MODEL), lambda i: (layer, 0, 0)),
            pl.BlockSpec((1, 1, D_MODEL), lambda i: (layer, 0, 0)),
            pl.BlockSpec(memory_space=pl.ANY),
            _layer_spec(conv_w.shape, j),
            pl.BlockSpec(memory_space=pl.ANY),
        ],
        out_specs=pl.BlockSpec((ROWS_MIX, D_MODEL), lambda i: (i, 0)),
        out_shape=jax.ShapeDtypeStruct(x2.shape, x2.dtype),
        scratch_shapes=([pltpu.VMEM((ROWS_MIX + hist, D_MODEL), jnp.float32)]
                        + _stage_scratch(w_in.shape) + _stage_scratch(w_out.shape)),
        compiler_params=pltpu.CompilerParams(
            dimension_semantics=("arbitrary",), vmem_limit_bytes=VMEM_LIMIT_BYTES),
        name="conv_layer",
    )(x2, mods, gain, w_in, conv_w, w_out)


def _ffn_kernel(x_ref, mods_ref, gain_ref, win_hbm, wout_hbm, fg_ref, o_ref,
                win_ref, win_stage, win_sems, wout_ref, wout_stage, wout_sems, *store_scratch,
                final_norm, n_blocks, w_layer):
    hidden = wout_ref.shape[0]

    @pl.when(pl.program_id(0) == 0)
    def _():
        _stage_bf16(win_hbm, w_layer, win_ref, win_stage, win_sems)
        _stage_bf16(wout_hbm, w_layer, wout_ref, wout_stage, wout_sems)

    x = x_ref[...]
    mods = mods_ref[0]
    shift, scale, gate = (mods[:, k * D_MODEL:(k + 1) * D_MODEL] for k in range(3, 6))
    h = _norm_modulate(x, gain_ref[0], shift, scale)
    gu = jnp.dot(h.astype(jnp.bfloat16), win_ref[...], preferred_element_type=jnp.float32)
    act = (jax.nn.silu(gu[:, :hidden]) * gu[:, hidden:]).astype(jnp.bfloat16)
    out = jnp.dot(act, wout_ref[...], preferred_element_type=jnp.float32)
    xn = _gated_residual(x, gate, out)
    if final_norm:
        xn = xn * lax.rsqrt(jnp.mean(xn * xn, axis=-1, keepdims=True) + RMS_EPS) * fg_ref[...]
    if store_scratch:
        _store_batch_major(xn, o_ref, *store_scratch, n_blocks)
    else:
        o_ref[...] = xn


def _ffn_layer(x2, mods, layer, gain, w_in, w_out, final_g, final_norm, batch_major_out):
    rows = x2.shape[0]
    n_blocks = rows // ROWS_FFN
    if batch_major_out:
        steps = ROWS_FFN // BATCH
        out_spec = pl.BlockSpec(memory_space=pl.ANY)
        out_shape = jax.ShapeDtypeStruct((BATCH, rows // BATCH, D_MODEL), x2.dtype)
        store_scratch = [pltpu.VMEM((2, steps, BATCH, D_MODEL), jnp.float32),
                         pltpu.SemaphoreType.DMA((2, BATCH))]
    else:
        out_spec = pl.BlockSpec((ROWS_FFN, D_MODEL), lambda i: (i, 0))
        out_shape = jax.ShapeDtypeStruct(x2.shape, x2.dtype)
        store_scratch = []
    return pl.pallas_call(
        functools.partial(_ffn_kernel, final_norm=final_norm, n_blocks=n_blocks, w_layer=layer),
        grid=(n_blocks,),
        in_specs=[
            pl.BlockSpec((ROWS_FFN, D_MODEL), lambda i: (i, 0)),
            pl.BlockSpec((1, BATCH, 6 * D_MODEL), lambda i: (layer, 0, 0)),
            pl.BlockSpec((1, 1, D_MODEL), lambda i: (layer, 0, 0)),
            pl.BlockSpec(memory_space=pl.ANY),
            pl.BlockSpec(memory_space=pl.ANY),
            _const_spec((1, D_MODEL)),
        ],
        out_specs=out_spec,
        out_shape=out_shape,
        scratch_shapes=_stage_scratch(w_in.shape) + _stage_scratch(w_out.shape) + store_scratch,
        compiler_params=pltpu.CompilerParams(
            dimension_semantics=("arbitrary",), vmem_limit_bytes=VMEM_LIMIT_BYTES),
        name="ffn_layer",
    )(x2, mods, gain, w_in, w_out, final_g.reshape(1, D_MODEL))


def kernel(x, c, norm1_g, norm2_g, w_ada, b_ada, ssm_a_re, ssm_a_im, ssm_log_step, ssm_b_re, ssm_b_im, ssm_c_re, ssm_c_im, ssm_d, ssm_w_out, conv_w_in, conv_w, conv_w_out, w_ffn_in, w_ffn_out, final_g):
    mods = _ada_mods(c, w_ada, b_ada)
    gain1 = norm1_g.reshape(DEPTH, 1, D_MODEL)
    gain2 = norm2_g.reshape(DEPTH, 1, D_MODEL)
    wz, wy, a2 = _ssm_prep(ssm_a_re, ssm_a_im, ssm_log_step, ssm_b_re, ssm_b_im,
                          ssm_c_re, ssm_c_im)
    ssm_d3 = ssm_d.reshape(-1, 1, D_MODEL)
    x2 = x
    for i in range(DEPTH):
        j = i // N_MIXERS
        if i % N_MIXERS == 0:
            x2 = _s5_layer(x2, mods, i, j, gain1, wz, wy, a2, ssm_d3, ssm_w_out)
        else:
            x2 = _conv_layer(x2, mods, i, j, gain1, conv_w_in, conv_w, conv_w_out)
        last = i == DEPTH - 1
        x2 = _ffn_layer(x2, mods, i, gain2, w_ffn_in, w_ffn_out, final_g,
                        final_norm=last, batch_major_out=last)
    return x2
```

```python
import functools

import jax
import jax.numpy as jnp
from jax import lax
from jax.experimental import pallas as pl
from jax.experimental.pallas import tpu as pltpu

D_MODEL = 1024
BATCH = 8
DEPTH = 4
N_MIXERS = 2
SSM_GROUP = 16
SSM_GROUPS = D_MODEL // SSM_GROUP
SSM_STATE = 64
CONV_WIDTH = 3
RMS_EPS = 1e-6

LANES = 128
SUBLANES = 8
BF16_ROWS = 16
VMEM_LIMIT_BYTES = 56 * 1024 * 1024
STAGE_BYTES = 3 * 512 * 1024

N_CHUNKS = SSM_GROUPS * SSM_STATE // LANES
GROUPS_PER_CHUNK = LANES // SSM_STATE
N_COLS = D_MODEL // LANES
GROUPS_PER_COL = LANES // SSM_GROUP
CHUNKS_PER_COL = N_CHUNKS // N_COLS
COL_K = CHUNKS_PER_COL * 2 * LANES + LANES

ROWS_FFN = 1024
ROWS_CONV = 1024
ROWS_S5 = 512
ADA_COLS = 3072


def _const_spec(shape, index=None):
    index = (0,) * len(shape) if index is None else index
    return pl.BlockSpec(shape, lambda *_: index, pipeline_mode=pl.Buffered(1))


def _layer_spec(shape, layer):
    return _const_spec((1,) + tuple(shape[1:]), (layer,) + (0,) * (len(shape) - 1))


def _norm_modulate(x, gain, shift, scale):
    rows = x.shape[0]
    y = x * lax.rsqrt(jnp.mean(x * x, axis=-1, keepdims=True) + RMS_EPS)
    y = y.reshape(rows // BATCH, BATCH, D_MODEL)
    return (y * (gain * (1.0 + scale))[None] + shift[None]).reshape(rows, D_MODEL)


def _gated_residual(x, gate, update):
    rows = x.shape[0]
    upd = update.reshape(rows // BATCH, BATCH, D_MODEL) * gate[None]
    return x + upd.reshape(rows, D_MODEL)


def _stage_scratch(shape):
    k_dim, n_dim = shape[-2:]
    fits = [r for r in range(BF16_ROWS, k_dim + 1, BF16_ROWS)
            if k_dim % r == 0 and r * n_dim * 4 <= STAGE_BYTES]
    return [pltpu.VMEM((k_dim, n_dim), jnp.bfloat16),
            pltpu.VMEM((2, max(fits), n_dim), jnp.float32),
            pltpu.SemaphoreType.DMA((2,))]


def _stage_bf16(w_hbm, layer, w_bf16, stage, sems):
    chunk = stage.shape[1]
    n_chunks = w_bf16.shape[0] // chunk

    def copy(k):
        return pltpu.make_async_copy(w_hbm.at[layer, pl.ds(k * chunk, chunk), :],
                                     stage.at[k % 2], sems.at[k % 2])

    copy(0).start()
    for k in range(n_chunks):
        if k + 1 < n_chunks:
            copy(k + 1).start()
        copy(k).wait()
        w_bf16[pl.ds(k * chunk, chunk), :] = stage[k % 2].astype(jnp.bfloat16)


def _batch_row_copies(hbm, buf, sems, block, slot, to_hbm):
    steps = buf.shape[1]
    copies = []
    for b in range(BATCH):
        hbm_rows = hbm.at[b, pl.ds(block * steps, steps), :]
        vmem_rows = buf.at[slot, :, b, :]
        src, dst = (vmem_rows, hbm_rows) if to_hbm else (hbm_rows, vmem_rows)
        copies.append(pltpu.make_async_copy(src, dst, sems.at[slot, b]))
    return copies


def _fetch_time_major(x_hbm, buf, sems, n_blocks):
    i = pl.program_id(0)

    @pl.when(i == 0)
    def _():
        for c in _batch_row_copies(x_hbm, buf, sems, 0, 0, to_hbm=False):
            c.start()

    @pl.when(i + 1 < n_blocks)
    def _():
        for c in _batch_row_copies(x_hbm, buf, sems, i + 1, (i + 1) % 2, to_hbm=False):
            c.start()

    for c in _batch_row_copies(x_hbm, buf, sems, i, i % 2, to_hbm=False):
        c.wait()
    return buf[i % 2].reshape(buf.shape[1] * BATCH, D_MODEL)


def _store_batch_major(value, o_hbm, buf, sems, n_blocks):
    i = pl.program_id(0)
    slot = i % 2

    @pl.when(i >= 2)
    def _():
        for c in _batch_row_copies(o_hbm, buf, sems, i - 2, slot, to_hbm=True):
            c.wait()

    buf[slot] = value.reshape(buf.shape[1], BATCH, D_MODEL)
    for c in _batch_row_copies(o_hbm, buf, sems, i, slot, to_hbm=True):
        c.start()

    @pl.when(i == n_blocks - 1)
    def _():
        for back in range(min(2, n_blocks)):
            for c in _batch_row_copies(o_hbm, buf, sems, i - back, (i - back) % 2, to_hbm=True):
                c.wait()


def _ada_kernel(c_ref, w_ref, b_ref, o_ref):
    c_act = jax.nn.silu(c_ref[...])
    o_ref[0] = jnp.dot(c_act, w_ref[0], preferred_element_type=jnp.float32) + b_ref[0]


def _ada_mods(c, w_ada, b_ada):
    n_cols = 6 * D_MODEL
    return pl.pallas_call(
        _ada_kernel,
        grid=(DEPTH, n_cols // ADA_COLS),
        in_specs=[
            pl.BlockSpec((BATCH, D_MODEL), lambda i, j: (0, 0)),
            pl.BlockSpec((1, D_MODEL, ADA_COLS), lambda i, j: (i, 0, j)),
            pl.BlockSpec((1, 1, ADA_COLS), lambda i, j: (i, 0, j)),
        ],
        out_specs=pl.BlockSpec((1, BATCH, ADA_COLS), lambda i, j: (i, 0, j)),
        out_shape=jax.ShapeDtypeStruct((DEPTH, BATCH, n_cols), jnp.float32),
        compiler_params=pltpu.CompilerParams(
            dimension_semantics=("arbitrary", "arbitrary"),
            vmem_limit_bytes=VMEM_LIMIT_BYTES),
        name="ada_mods",
    )(c, w_ada, b_ada.reshape(DEPTH, 1, n_cols))


def _zoh(a_re, a_im, log_step):
    lr = jnp.minimum(a_re, -1e-4)
    li = a_im
    dt = jnp.exp(log_step)
    mag = jnp.exp(lr * dt)
    abr = mag * jnp.cos(li * dt)
    abi = mag * jnp.sin(li * dt)
    den = lr * lr + li * li
    qr = ((abr - 1.0) * lr + abi * li) / den
    qi = (abi * lr - (abr - 1.0) * li) / den
    return abr, abi, qr, qi


def _ssm_param_kernel(are_ref, aim_ref, ls_ref, btr_ref, bti_ref,
                      aret_ref, aimt_ref, lst_ref, ctr_ref, cti_ref,
                      wz_ref, wy_ref, a2_ref):
    bf16 = jnp.bfloat16
    abr, abi, qr, qi = _zoh(are_ref[0], aim_ref[0], ls_ref[0])
    a2r = abr * abr - abi * abi
    a2i = 2.0 * abr * abi
    a2_ref[0, :, 0] = jnp.broadcast_to(a2r[:, None, :], (N_CHUNKS, SUBLANES, LANES))
    a2_ref[0, :, 1] = jnp.broadcast_to(a2i[:, None, :], (N_CHUNKS, SUBLANES, LANES))
    btr = btr_ref[0]
    bti = bti_ref[0]
    bbr = qr[:, None, :] * btr - qi[:, None, :] * bti
    bbi = qr[:, None, :] * bti + qi[:, None, :] * btr
    bar = abr[:, None, :] * bbr - abi[:, None, :] * bbi
    bai = abr[:, None, :] * bbi + abi[:, None, :] * bbr
    abr_t, abi_t, _, _ = _zoh(aret_ref[0], aimt_ref[0], lst_ref[0])
    ctr = ctr_ref[0]
    cti = cti_ref[0]

    row_group = lax.broadcasted_iota(jnp.int32, (LANES, LANES), 0) // SSM_GROUP
    col_group = lax.broadcasted_iota(jnp.int32, (LANES, LANES), 1) // SSM_GROUP
    row_sub = lax.broadcasted_iota(jnp.int32, (LANES, LANES), 0) // SSM_STATE
    col_sub = lax.broadcasted_iota(jnp.int32, (LANES, LANES), 1) // SSM_STATE
    lane_sub = lax.broadcasted_iota(jnp.int32, (SSM_GROUP, LANES), 1) // SSM_STATE
    spread = (lax.broadcasted_iota(jnp.int32, (SSM_GROUP, LANES), 1) % SSM_GROUP
              == lax.broadcasted_iota(jnp.int32, (SSM_GROUP, LANES), 0)).astype(bf16)

    def rows_by_group(block):
        rep = jnp.broadcast_to(block[None], (GROUPS_PER_COL, SSM_GROUP, block.shape[-1]))
        return rep.reshape(LANES, block.shape[-1])

    def cols_by_group(block):
        return jnp.dot(block.astype(bf16), spread, preferred_element_type=jnp.float32)

    for j in range(N_CHUNKS):
        v, c = divmod(j, CHUNKS_PER_COL)
        b_live = row_group == c * GROUPS_PER_CHUNK + col_sub
        for half, (re, im) in enumerate(((bar, bai), (bbr, bbi))):
            rows = pl.ds(half * LANES, LANES)
            wz_ref[0, j, rows, pl.ds(0, LANES)] = (
                jnp.where(b_live, rows_by_group(re[j]), 0.0).astype(bf16))
            wz_ref[0, j, rows, pl.ds(LANES, LANES)] = (
                jnp.where(b_live, rows_by_group(im[j]), 0.0).astype(bf16))
        c_live = col_group == c * GROUPS_PER_CHUNK + row_sub
        a_r, a_i = abr_t[:, j:j + 1], abi_t[:, j:j + 1]
        car = ctr[j] * a_r - cti[j] * a_i
        cai = ctr[j] * a_i + cti[j] * a_r
        base = c * 2 * LANES
        for part, (even, odd) in enumerate(((ctr[j], car), (-cti[j], -cai))):
            rows = pl.ds(base + part * LANES, LANES)
            wy_ref[0, v, rows, pl.ds(0, LANES)] = (
                jnp.where(c_live, cols_by_group(even), 0.0).astype(bf16))
            wy_ref[0, v, rows, pl.ds(LANES, LANES)] = (
                jnp.where(c_live, cols_by_group(odd), 0.0).astype(bf16))

    for v in range(N_COLS):
        acc = jnp.zeros((LANES, LANES), jnp.float32)
        for c in range(CHUNKS_PER_COL):
            j = v * CHUNKS_PER_COL + c
            for l in range(GROUPS_PER_CHUNK):
                mine = lane_sub == l
                cb = (jnp.dot(jnp.where(mine, bbr[j], 0.0), ctr[j],
                              preferred_element_type=jnp.float32)
                      - jnp.dot(jnp.where(mine, bbi[j], 0.0), cti[j],
                                preferred_element_type=jnp.float32))
                group = c * GROUPS_PER_CHUNK + l
                on_diag = (row_group == group) & (col_group == group)
                acc = acc + jnp.where(on_diag, cols_by_group(rows_by_group(cb)), 0.0)
        rows = pl.ds(CHUNKS_PER_COL * 2 * LANES, LANES)
        wy_ref[0, v, rows, pl.ds(0, LANES)] = jnp.zeros((LANES, LANES), bf16)
        wy_ref[0, v, rows, pl.ds(LANES, LANES)] = acc.astype(bf16)


def _ssm_prep(a_re, a_im, log_step, b_re, b_im, c_re, c_im):
    n = a_re.shape[0]
    f32 = jnp.float32
    g, p, h = SSM_GROUPS, SSM_STATE, SSM_GROUP

    def chunk_lanes(v):
        return v.reshape(n, N_CHUNKS, LANES)

    def chunk_rows(v):
        return jnp.swapaxes(chunk_lanes(v), 1, 2)

    def b_blocks(v):
        v = v.reshape(n, N_CHUNKS, GROUPS_PER_CHUNK, p, h)
        return jnp.transpose(v, (0, 1, 4, 2, 3)).reshape(n, N_CHUNKS, h, LANES)

    def c_blocks(v):
        v = v.reshape(n, N_CHUNKS, GROUPS_PER_CHUNK, h, p)
        return jnp.transpose(v, (0, 1, 2, 4, 3)).reshape(n, N_CHUNKS, LANES, h)

    ls = jnp.broadcast_to(log_step[:, :, None], (n, g, p))
    lane_spec = pl.BlockSpec((1, N_CHUNKS, LANES), lambda i: (i, 0, 0))
    b_spec = pl.BlockSpec((1, N_CHUNKS, h, LANES), lambda i: (i, 0, 0, 0))
    row_spec = pl.BlockSpec((1, LANES, N_CHUNKS), lambda i: (i, 0, 0))
    c_spec = pl.BlockSpec((1, N_CHUNKS, LANES, h), lambda i: (i, 0, 0, 0))
    return pl.pallas_call(
        _ssm_param_kernel,
        grid=(n,),
        in_specs=[lane_spec, lane_spec, lane_spec, b_spec, b_spec,
                  row_spec, row_spec, row_spec, c_spec, c_spec],
        out_specs=(
            pl.BlockSpec((1, N_CHUNKS, 2 * LANES, 2 * LANES), lambda i: (i, 0, 0, 0)),
            pl.BlockSpec((1, N_COLS, COL_K, 2 * LANES), lambda i: (i, 0, 0, 0)),
            pl.BlockSpec((1, N_CHUNKS, 2, SUBLANES, LANES), lambda i: (i, 0, 0, 0, 0)),
        ),
        out_shape=(
            jax.ShapeDtypeStruct((n, N_CHUNKS, 2 * LANES, 2 * LANES), jnp.bfloat16),
            jax.ShapeDtypeStruct((n, N_COLS, COL_K, 2 * LANES), jnp.bfloat16),
            jax.ShapeDtypeStruct((n, N_CHUNKS, 2, SUBLANES, LANES), f32),
        ),
        compiler_params=pltpu.CompilerParams(
            dimension_semantics=("arbitrary",), vmem_limit_bytes=VMEM_LIMIT_BYTES),
        name="ssm_params",
    )(chunk_lanes(a_re), chunk_lanes(a_im), chunk_lanes(ls), b_blocks(b_re), b_blocks(b_im),
      chunk_rows(a_re), chunk_rows(a_im), chunk_rows(ls), c_blocks(c_re), c_blocks(c_im))


def _s5_kernel(x_ref, mods_ref, gain_ref, wz_ref, wy_ref, a2_ref, d_ref, wout_hbm,
               o_ref, z_ref, xs_ref, state_ref, oprev_ref, wout_ref, wout_stage, wout_sems,
               *fetch_scratch, pairs, n_blocks, w_layer):
    bf16 = jnp.bfloat16

    @pl.when(pl.program_id(0) == 0)
    def _():
        state_ref[...] = jnp.zeros_like(state_ref)
        oprev_ref[...] = jnp.zeros_like(oprev_ref)
        _stage_bf16(wout_hbm, w_layer, wout_ref, wout_stage, wout_sems)

    if fetch_scratch:
        x = _fetch_time_major(x_ref, *fetch_scratch, n_blocks)
    else:
        x = x_ref[...]
    mods = mods_ref[0]
    shift, scale, gate = (mods[:, k * D_MODEL:(k + 1) * D_MODEL] for k in range(3))

    half_rows = pairs * BATCH
    x_pairs = x.reshape(pairs, 2, BATCH, D_MODEL)
    x_even = x_pairs[:, 0].reshape(half_rows, D_MODEL)
    x_odd = x_pairs[:, 1].reshape(half_rows, D_MODEL)
    h_even = _norm_modulate(x_even, gain_ref[0], shift, scale)
    h_odd = _norm_modulate(x_odd, gain_ref[0], shift, scale)
    h_odd_prev = jnp.concatenate([oprev_ref[...], h_odd[:half_rows - BATCH]], axis=0)
    oprev_ref[...] = h_odd[half_rows - BATCH:]
    hb_even, hb_odd, hb_odd_prev = (t.astype(bf16) for t in (h_even, h_odd, h_odd_prev))

    ys_even, ys_odd = [], []
    for v in range(N_COLS):
        lanes = slice(v * LANES, (v + 1) * LANES)
        chunks = range(v * CHUNKS_PER_COL, (v + 1) * CHUNKS_PER_COL)
        lhs = jnp.concatenate([hb_odd_prev[:, lanes], hb_even[:, lanes]], axis=-1)
        for j in chunks:
            z_ref[j] = jnp.dot(lhs, wz_ref[0, j], preferred_element_type=jnp.float32)

        carry = [(state_ref[j, 0], state_ref[j, 1]) for j in chunks]
        for m0 in range(0, pairs, BF16_ROWS // SUBLANES):
            for c, j in enumerate(chunks):
                sr, si = carry[c]
                a_re, a_im = a2_ref[0, j, 0], a2_ref[0, j, 1]
                outs_r, outs_i = [], []
                for m in range(m0, m0 + BF16_ROWS // SUBLANES):
                    rows = pl.ds(m * SUBLANES, SUBLANES)
                    z_r = z_ref[j, rows, pl.ds(0, LANES)]
                    z_i = z_ref[j, rows, pl.ds(LANES, LANES)]
                    sr, si = (a_re * sr - a_im * si + z_r, a_re * si + a_im * sr + z_i)
                    outs_r.append(sr)
                    outs_i.append(si)
                carry[c] = (sr, si)
                rows = pl.ds(m0 * SUBLANES, BF16_ROWS)
                col = c * 2 * LANES
                xs_ref[v, rows, pl.ds(col, LANES)] = jnp.concatenate(outs_r, axis=0).astype(bf16)
                xs_ref[v, rows, pl.ds(col + LANES, LANES)] = (
                    jnp.concatenate(outs_i, axis=0).astype(bf16))
        for c, j in enumerate(chunks):
            state_ref[j, 0] = carry[c][0]
            state_ref[j, 1] = carry[c][1]

        lhs = jnp.concatenate([xs_ref[v], hb_odd[:, lanes]], axis=-1)
        y = jnp.dot(lhs, wy_ref[0, v], preferred_element_type=jnp.float32)
        ys_even.append(y[:, :LANES])
        ys_odd.append(y[:, LANES:])

    y = jnp.concatenate([jnp.concatenate(ys_even, axis=-1) + d_ref[0] * h_even,
                         jnp.concatenate(ys_odd, axis=-1) + d_ref[0] * h_odd], axis=0)
    y = jax.nn.gelu(y).astype(bf16)
    z = jnp.dot(y, wout_ref[...], preferred_element_type=jnp.float32)
    mix = z[:, :D_MODEL] * jax.nn.sigmoid(z[:, D_MODEL:])
    out = _gated_residual(jnp.concatenate([x_even, x_odd], axis=0), gate, mix)
    out_pairs = jnp.concatenate([out[:half_rows].reshape(pairs, 1, BATCH, D_MODEL),
                                 out[half_rows:].reshape(pairs, 1, BATCH, D_MODEL)], axis=1)
    o_ref[...] = out_pairs.reshape(2 * half_rows, D_MODEL)


def _s5_layer(x_in, mods, layer, j, gain, wz, wy, a2, d, w_out):
    steps = ROWS_S5 // BATCH
    pairs = steps // 2
    batch_major = x_in.ndim == 3
    rows = x_in.shape[0] * x_in.shape[1] if batch_major else x_in.shape[0]
    n_blocks = rows // ROWS_S5
    if batch_major:
        x_spec = pl.BlockSpec(memory_space=pl.ANY)
        fetch_scratch = [pltpu.VMEM((2, steps, BATCH, D_MODEL), jnp.float32),
                         pltpu.SemaphoreType.DMA((2, BATCH))]
    else:
        x_spec = pl.BlockSpec((ROWS_S5, D_MODEL), lambda i: (i, 0))
        fetch_scratch = []
    return pl.pallas_call(
        functools.partial(_s5_kernel, pairs=pairs, n_blocks=n_blocks, w_layer=j),
        grid=(n_blocks,),
        in_specs=[
            x_spec,
            pl.BlockSpec((1, BATCH, 6 * D_MODEL), lambda i: (layer, 0, 0)),
            pl.BlockSpec((1, 1, D_MODEL), lambda i: (layer, 0, 0)),
            _layer_spec(wz.shape, j),
            _layer_spec(wy.shape, j),
            _layer_spec(a2.shape, j),
            _layer_spec(d.shape, j),
            pl.BlockSpec(memory_space=pl.ANY),
        ],
        out_specs=pl.BlockSpec((ROWS_S5, D_MODEL), lambda i: (i, 0)),
        out_shape=jax.ShapeDtypeStruct((rows, D_MODEL), x_in.dtype),
        scratch_shapes=[
            pltpu.VMEM((N_CHUNKS, pairs * BATCH, 2 * LANES), jnp.float32),
            pltpu.VMEM((N_COLS, pairs * BATCH, CHUNKS_PER_COL * 2 * LANES), jnp.bfloat16),
            pltpu.VMEM((N_CHUNKS, 2, SUBLANES, LANES), jnp.float32),
            pltpu.VMEM((BATCH, D_MODEL), jnp.float32),
        ] + _stage_scratch(w_out.shape) + fetch_scratch,
        compiler_params=pltpu.CompilerParams(
            dimension_semantics=("arbitrary",), vmem_limit_bytes=VMEM_LIMIT_BYTES),
        name="s5_layer",
    )(x_in, mods, gain, wz, wy, a2, d, w_out)


def _conv_kernel(x_ref, mods_ref, gain_ref, win_hbm, cw_ref, wout_hbm, o_ref, cv_ref,
                 win_ref, win_stage, win_sems, wout_ref, wout_stage, wout_sems, *, w_layer):
    rows = x_ref.shape[0]
    hist = (CONV_WIDTH - 1) * BATCH

    @pl.when(pl.program_id(0) == 0)
    def _():
        cv_ref[pl.ds(0, hist), :] = jnp.zeros((hist, D_MODEL), jnp.float32)
        _stage_bf16(win_hbm, w_layer, win_ref, win_stage, win_sems)
        _stage_bf16(wout_hbm, w_layer, wout_ref, wout_stage, wout_sems)

    x = x_ref[...]
    mods = mods_ref[0]
    shift, scale, gate = (mods[:, k * D_MODEL:(k + 1) * D_MODEL] for k in range(3))
    h = _norm_modulate(x, gain_ref[0], shift, scale)
    proj = jnp.dot(h.astype(jnp.bfloat16), win_ref[...], preferred_element_type=jnp.float32)
    bg = proj[:, :D_MODEL]
    cv = proj[:, D_MODEL:2 * D_MODEL] * proj[:, 2 * D_MODEL:]
    cv_ref[pl.ds(hist, rows), :] = cv
    cw = cw_ref[0]
    conv = (cw[2:3] * cv + cw[1:2] * cv_ref[pl.ds(BATCH, rows), :]
            + cw[0:1] * cv_ref[pl.ds(0, rows), :])
    cv_ref[pl.ds(0, hist), :] = cv_ref[pl.ds(rows, hist), :]
    out = jnp.dot((bg * conv).astype(jnp.bfloat16), wout_ref[...],
                  preferred_element_type=jnp.float32)
    o_ref[...] = _gated_residual(x, gate, out)


def _conv_layer(x2, mods, layer, j, gain, w_in, conv_w, w_out):
    rows = x2.shape[0]
    hist = (CONV_WIDTH - 1) * BATCH
    return pl.pallas_call(
        functools.partial(_conv_kernel, w_layer=j),
        grid=(rows // ROWS_CONV,),
        in_specs=[
            pl.BlockSpec((ROWS_CONV, D_MODEL), lambda i: (i, 0)),
            pl.BlockSpec((1, BATCH, 6 * D_MODEL), lambda i: (layer, 0, 0)),
            pl.BlockSpec((1, 1, D_MODEL), lambda i: (layer, 0, 0)),
            pl.BlockSpec(memory_space=pl.ANY),
            _layer_spec(conv_w.shape, j),
            pl.BlockSpec(memory_space=pl.ANY),
        ],
        out_specs=pl.BlockSpec((ROWS_CONV, D_MODEL), lambda i: (i, 0)),
        out_shape=jax.ShapeDtypeStruct(x2.shape, x2.dtype),
        scratch_shapes=([pltpu.VMEM((ROWS_CONV + hist, D_MODEL), jnp.float32)]
                        + _stage_scratch(w_in.shape) + _stage_scratch(w_out.shape)),
        compiler_params=pltpu.CompilerParams(
            dimension_semantics=("arbitrary",), vmem_limit_bytes=VMEM_LIMIT_BYTES),
        name="conv_layer",
    )(x2, mods, gain, w_in, conv_w, w_out)


def _ffn_kernel(x_ref, mods_ref, gain_ref, win_hbm, wout_hbm, fg_ref, o_ref,
                win_ref, win_stage, win_sems, wout_ref, wout_stage, wout_sems, *store_scratch,
                final_norm, n_blocks, w_layer):
    hidden = wout_ref.shape[0]

    @pl.when(pl.program_id(0) == 0)
    def _():
        _stage_bf16(win_hbm, w_layer, win_ref, win_stage, win_sems)
        _stage_bf16(wout_hbm, w_layer, wout_ref, wout_stage, wout_sems)

    mods = mods_ref[0]
    shift, scale, gate = (mods[:, k * D_MODEL:(k + 1) * D_MODEL] for k in range(3, 6))
    half_rows = x_ref.shape[0] // 2
    halves = []
    for half in range(2):
        rows = pl.ds(half * half_rows, half_rows)
        x = x_ref[rows, :]
        h = _norm_modulate(x, gain_ref[0], shift, scale)
        gu = jnp.dot(h.astype(jnp.bfloat16), win_ref[...], preferred_element_type=jnp.float32)
        act = (jax.nn.silu(gu[:, :hidden]) * gu[:, hidden:]).astype(jnp.bfloat16)
        out = jnp.dot(act, wout_ref[...], preferred_element_type=jnp.float32)
        xn = _gated_residual(x, gate, out)
        if final_norm:
            xn = xn * lax.rsqrt(jnp.mean(xn * xn, axis=-1, keepdims=True) + RMS_EPS) * fg_ref[...]
        if store_scratch:
            halves.append(xn)
        else:
            o_ref[rows, :] = xn
    if store_scratch:
        _store_batch_major(jnp.concatenate(halves, axis=0), o_ref, *store_scratch, n_blocks)


def _ffn_layer(x2, mods, layer, gain, w_in, w_out, final_g, final_norm, batch_major_out):
    rows = x2.shape[0]
    n_blocks = rows // ROWS_FFN
    if batch_major_out:
        steps = ROWS_FFN // BATCH
        out_spec = pl.BlockSpec(memory_space=pl.ANY)
        out_shape = jax.ShapeDtypeStruct((BATCH, rows // BATCH, D_MODEL), x2.dtype)
        store_scratch = [pltpu.VMEM((2, steps, BATCH, D_MODEL), jnp.float32),
                         pltpu.SemaphoreType.DMA((2, BATCH))]
    else:
        out_spec = pl.BlockSpec((ROWS_FFN, D_MODEL), lambda i: (i, 0))
        out_shape = jax.ShapeDtypeStruct(x2.shape, x2.dtype)
        store_scratch = []
    return pl.pallas_call(
        functools.partial(_ffn_kernel, final_norm=final_norm, n_blocks=n_blocks, w_layer=layer),
        grid=(n_blocks,),
        in_specs=[
            pl.BlockSpec((ROWS_FFN, D_MODEL), lambda i: (i, 0)),
            pl.BlockSpec((1, BATCH, 6 * D_MODEL), lambda i: (layer, 0, 0)),
            pl.BlockSpec((1, 1, D_MODEL), lambda i: (layer, 0, 0)),
            pl.BlockSpec(memory_space=pl.ANY),
            pl.BlockSpec(memory_space=pl.ANY),
            _const_spec((1, D_MODEL)),
        ],
        out_specs=out_spec,
        out_shape=out_shape,
        scratch_shapes=_stage_scratch(w_in.shape) + _stage_scratch(w_out.shape) + store_scratch,
        compiler_params=pltpu.CompilerParams(
            dimension_semantics=("arbitrary",), vmem_limit_bytes=VMEM_LIMIT_BYTES),
        name="ffn_layer",
    )(x2, mods, gain, w_in, w_out, final_g.reshape(1, D_MODEL))


def kernel(x, c, norm1_g, norm2_g, w_ada, b_ada, ssm_a_re, ssm_a_im, ssm_log_step, ssm_b_re, ssm_b_im, ssm_c_re, ssm_c_im, ssm_d, ssm_w_out, conv_w_in, conv_w, conv_w_out, w_ffn_in, w_ffn_out, final_g):
    mods = _ada_mods(c, w_ada, b_ada)
    gain1 = norm1_g.reshape(DEPTH, 1, D_MODEL)
    gain2 = norm2_g.reshape(DEPTH, 1, D_MODEL)
    wz, wy, a2 = _ssm_prep(ssm_a_re, ssm_a_im, ssm_log_step, ssm_b_re, ssm_b_im,
                          ssm_c_re, ssm_c_im)
    ssm_d3 = ssm_d.reshape(-1, 1, D_MODEL)
    x2 = x
    for i in range(DEPTH):
        j = i // N_MIXERS
        if i % N_MIXERS == 0:
            x2 = _s5_layer(x2, mods, i, j, gain1, wz, wy, a2, ssm_d3, ssm_w_out)
        else:
            x2 = _conv_layer(x2, mods, i, j, gain1, conv_w_in, conv_w, conv_w_out)
        last = i == DEPTH - 1
        x2 = _ffn_layer(x2, mods, i, gain2, w_ffn_in, w_ffn_out, final_g,
                        final_norm=last, batch_major_out=last)
    return x2
```

```python
import functools

import jax
import jax.numpy as jnp
from jax import lax
from jax.experimental import pallas as pl
from jax.experimental.pallas import tpu as pltpu

D_MODEL = 1024
BATCH = 8
DEPTH = 4
N_MIXERS = 2
SSM_GROUP = 16
SSM_GROUPS = D_MODEL // SSM_GROUP
SSM_STATE = 64
CONV_WIDTH = 3
RMS_EPS = 1e-6

LANES = 128
SUBLANES = 8
BF16_ROWS = 16
VMEM_LIMIT_BYTES = 56 * 1024 * 1024
STAGE_BYTES = 3 * 512 * 1024

N_CHUNKS = SSM_GROUPS * SSM_STATE // LANES
GROUPS_PER_CHUNK = LANES // SSM_STATE
N_COLS = D_MODEL // LANES
GROUPS_PER_COL = LANES // SSM_GROUP
CHUNKS_PER_COL = N_CHUNKS // N_COLS
COL_K = CHUNKS_PER_COL * 2 * LANES + LANES

ROWS_FFN = 1024
ROWS_CONV = 1024
ROWS_S5 = 512
ADA_COLS = 3072


def _const_spec(shape, index=None):
    index = (0,) * len(shape) if index is None else index
    return pl.BlockSpec(shape, lambda *_: index, pipeline_mode=pl.Buffered(1))


def _layer_spec(shape, layer):
    return _const_spec((1,) + tuple(shape[1:]), (layer,) + (0,) * (len(shape) - 1))


def _norm_modulate(x, gain, shift, scale):
    rows = x.shape[0]
    y = x * lax.rsqrt(jnp.mean(x * x, axis=-1, keepdims=True) + RMS_EPS)
    y = y.reshape(rows // BATCH, BATCH, D_MODEL)
    return (y * (gain * (1.0 + scale))[None] + shift[None]).reshape(rows, D_MODEL)


def _gated_residual(x, gate, update):
    rows = x.shape[0]
    upd = update.reshape(rows // BATCH, BATCH, D_MODEL) * gate[None]
    return x + upd.reshape(rows, D_MODEL)


def _stage_scratch(shape):
    k_dim, n_dim = shape[-2:]
    fits = [r for r in range(BF16_ROWS, k_dim + 1, BF16_ROWS)
            if k_dim % r == 0 and r * n_dim * 4 <= STAGE_BYTES]
    return [pltpu.VMEM((k_dim, n_dim), jnp.bfloat16),
            pltpu.VMEM((2, max(fits), n_dim), jnp.float32),
            pltpu.SemaphoreType.DMA((2,))]


def _stage_bf16(w_hbm, layer, w_bf16, stage, sems):
    chunk = stage.shape[1]
    n_chunks = w_bf16.shape[0] // chunk

    def copy(k):
        return pltpu.make_async_copy(w_hbm.at[layer, pl.ds(k * chunk, chunk), :],
                                     stage.at[k % 2], sems.at[k % 2])

    copy(0).start()
    for k in range(n_chunks):
        if k + 1 < n_chunks:
            copy(k + 1).start()
        copy(k).wait()
        w_bf16[pl.ds(k * chunk, chunk), :] = stage[k % 2].astype(jnp.bfloat16)


def _batch_row_copies(hbm, buf, sems, block, slot, to_hbm):
    steps = buf.shape[1]
    copies = []
    for b in range(BATCH):
        hbm_rows = hbm.at[b, pl.ds(block * steps, steps), :]
        vmem_rows = buf.at[slot, :, b, :]
        src, dst = (vmem_rows, hbm_rows) if to_hbm else (hbm_rows, vmem_rows)
        copies.append(pltpu.make_async_copy(src, dst, sems.at[slot, b]))
    return copies


def _fetch_time_major(x_hbm, buf, sems, n_blocks):
    i = pl.program_id(0)

    @pl.when(i == 0)
    def _():
        for c in _batch_row_copies(x_hbm, buf, sems, 0, 0, to_hbm=False):
            c.start()

    @pl.when(i + 1 < n_blocks)
    def _():
        for c in _batch_row_copies(x_hbm, buf, sems, i + 1, (i + 1) % 2, to_hbm=False):
            c.start()

    for c in _batch_row_copies(x_hbm, buf, sems, i, i % 2, to_hbm=False):
        c.wait()
    return buf[i % 2].reshape(buf.shape[1] * BATCH, D_MODEL)


def _store_batch_major(value, o_hbm, buf, sems, n_blocks):
    i = pl.program_id(0)
    slot = i % 2

    @pl.when(i >= 2)
    def _():
        for c in _batch_row_copies(o_hbm, buf, sems, i - 2, slot, to_hbm=True):
            c.wait()

    buf[slot] = value.reshape(buf.shape[1], BATCH, D_MODEL)
    for c in _batch_row_copies(o_hbm, buf, sems, i, slot, to_hbm=True):
        c.start()

    @pl.when(i == n_blocks - 1)
    def _():
        for back in range(min(2, n_blocks)):
            for c in _batch_row_copies(o_hbm, buf, sems, i - back, (i - back) % 2, to_hbm=True):
                c.wait()


def _ada_kernel(c_ref, w_ref, b_ref, o_ref):
    c_act = jax.nn.silu(c_ref[...])
    o_ref[0] = jnp.dot(c_act, w_ref[0], preferred_element_type=jnp.float32) + b_ref[0]


def _ada_mods(c, w_ada, b_ada):
    n_cols = 6 * D_MODEL
    return pl.pallas_call(
        _ada_kernel,
        grid=(DEPTH, n_cols // ADA_COLS),
        in_specs=[
            pl.BlockSpec((BATCH, D_MODEL), lambda i, j: (0, 0)),
            pl.BlockSpec((1, D_MODEL, ADA_COLS), lambda i, j: (i, 0, j)),
            pl.BlockSpec((1, 1, ADA_COLS), lambda i, j: (i, 0, j)),
        ],
        out_specs=pl.BlockSpec((1, BATCH, ADA_COLS), lambda i, j: (i, 0, j)),
        out_shape=jax.ShapeDtypeStruct((DEPTH, BATCH, n_cols), jnp.float32),
        compiler_params=pltpu.CompilerParams(
            dimension_semantics=("arbitrary", "arbitrary"),
            vmem_limit_bytes=VMEM_LIMIT_BYTES),
        name="ada_mods",
    )(c, w_ada, b_ada.reshape(DEPTH, 1, n_cols))


def _zoh(a_re, a_im, log_step):
    lr = jnp.minimum(a_re, -1e-4)
    li = a_im
    dt = jnp.exp(log_step)
    mag = jnp.exp(lr * dt)
    abr = mag * jnp.cos(li * dt)
    abi = mag * jnp.sin(li * dt)
    den = lr * lr + li * li
    qr = ((abr - 1.0) * lr + abi * li) / den
    qi = (abi * lr - (abr - 1.0) * li) / den
    return abr, abi, qr, qi


def _ssm_param_kernel(al_ref, at_ref, bt_ref, ct_ref, wz_ref, wy_ref, a2_ref):
    bf16 = jnp.bfloat16
    abr, abi, qr, qi = _zoh(al_ref[0, 0], al_ref[1, 0], al_ref[2, 0])
    a2r = abr * abr - abi * abi
    a2i = 2.0 * abr * abi
    a2_ref[0, :, 0] = jnp.broadcast_to(a2r[:, None, :], (N_CHUNKS, SUBLANES, LANES))
    a2_ref[0, :, 1] = jnp.broadcast_to(a2i[:, None, :], (N_CHUNKS, SUBLANES, LANES))
    btr = bt_ref[0, 0]
    bti = bt_ref[1, 0]
    bbr = qr[:, None, :] * btr - qi[:, None, :] * bti
    bbi = qr[:, None, :] * bti + qi[:, None, :] * btr
    bar = abr[:, None, :] * bbr - abi[:, None, :] * bbi
    bai = abr[:, None, :] * bbi + abi[:, None, :] * bbr
    abr_t, abi_t, _, _ = _zoh(at_ref[0, 0], at_ref[1, 0], at_ref[2, 0])
    ctr = ct_ref[0, 0]
    cti = ct_ref[1, 0]

    row_group = lax.broadcasted_iota(jnp.int32, (LANES, LANES), 0) // SSM_GROUP
    col_group = lax.broadcasted_iota(jnp.int32, (LANES, LANES), 1) // SSM_GROUP
    row_sub = lax.broadcasted_iota(jnp.int32, (LANES, LANES), 0) // SSM_STATE
    col_sub = lax.broadcasted_iota(jnp.int32, (LANES, LANES), 1) // SSM_STATE
    lane_sub = lax.broadcasted_iota(jnp.int32, (SSM_GROUP, LANES), 1) // SSM_STATE
    spread = (lax.broadcasted_iota(jnp.int32, (SSM_GROUP, LANES), 1) % SSM_GROUP
              == lax.broadcasted_iota(jnp.int32, (SSM_GROUP, LANES), 0)).astype(bf16)

    def rows_by_group(block):
        rep = jnp.broadcast_to(block[None], (GROUPS_PER_COL, SSM_GROUP, block.shape[-1]))
        return rep.reshape(LANES, block.shape[-1])

    def cols_by_group(blocks):
        flat = blocks.reshape(-1, SSM_GROUP).astype(bf16)
        return jnp.dot(flat, spread, preferred_element_type=jnp.float32)

    for j in range(N_CHUNKS):
        c = j % CHUNKS_PER_COL
        b_live = row_group == c * GROUPS_PER_CHUNK + col_sub
        for half, (re, im) in enumerate(((bar, bai), (bbr, bbi))):
            rows = pl.ds(half * LANES, LANES)
            wz_ref[0, j, rows, pl.ds(0, LANES)] = (
                jnp.where(b_live, rows_by_group(re[j]), 0.0).astype(bf16))
            wz_ref[0, j, rows, pl.ds(LANES, LANES)] = (
                jnp.where(b_live, rows_by_group(im[j]), 0.0).astype(bf16))

    car, cai = [], []
    for j in range(N_CHUNKS):
        a_r, a_i = abr_t[:, j:j + 1], abi_t[:, j:j + 1]
        car.append(ctr[j] * a_r - cti[j] * a_i)
        cai.append(ctr[j] * a_i + cti[j] * a_r)
    car, cai = jnp.stack(car), jnp.stack(cai)
    spread_c = [(cols_by_group(ctr), cols_by_group(car)), (cols_by_group(-cti), cols_by_group(-cai))]
    for j in range(N_CHUNKS):
        v, c = divmod(j, CHUNKS_PER_COL)
        c_live = col_group == c * GROUPS_PER_CHUNK + row_sub
        for part, (even, odd) in enumerate(spread_c):
            rows = pl.ds(c * 2 * LANES + part * LANES, LANES)
            src_rows = slice(j * LANES, (j + 1) * LANES)
            wy_ref[0, v, rows, pl.ds(0, LANES)] = jnp.where(c_live, even[src_rows], 0.0).astype(bf16)
            wy_ref[0, v, rows, pl.ds(LANES, LANES)] = jnp.where(c_live, odd[src_rows], 0.0).astype(bf16)

    cb = []
    for l in range(GROUPS_PER_CHUNK):
        mine = (lane_sub == l)[None]
        cb.append(jnp.einsum('jhk,jko->jho', jnp.where(mine, bbr, 0.0), ctr,
                             preferred_element_type=jnp.float32)
                  - jnp.einsum('jhk,jko->jho', jnp.where(mine, bbi, 0.0), cti,
                               preferred_element_type=jnp.float32))
    cb = cols_by_group(jnp.stack(cb, axis=1))
    on_diag = row_group == col_group
    for v in range(N_COLS):
        rows = pl.ds(CHUNKS_PER_COL * 2 * LANES, LANES)
        wy_ref[0, v, rows, pl.ds(0, LANES)] = jnp.zeros((LANES, LANES), bf16)
        wy_ref[0, v, rows, pl.ds(LANES, LANES)] = (
            jnp.where(on_diag, cb[v * LANES:(v + 1) * LANES], 0.0).astype(bf16))


def _ssm_prep(a_re, a_im, log_step, b_re, b_im, c_re, c_im):
    n = a_re.shape[0]
    f32 = jnp.float32
    p, h = SSM_STATE, SSM_GROUP
    a_lanes = jnp.stack([a_re, a_im, jnp.broadcast_to(log_step[:, :, None], a_re.shape)])
    a_lanes = a_lanes.reshape(3, n, N_CHUNKS, LANES)
    a_rows = jnp.swapaxes(a_lanes, 2, 3)
    b_t = jnp.stack([b_re, b_im]).reshape(2, n, N_CHUNKS, GROUPS_PER_CHUNK, p, h)
    b_t = jnp.transpose(b_t, (0, 1, 2, 5, 3, 4)).reshape(2, n, N_CHUNKS, h, LANES)
    c_t = jnp.stack([c_re, c_im]).reshape(2, n, N_CHUNKS, GROUPS_PER_CHUNK, h, p)
    c_t = jnp.transpose(c_t, (0, 1, 2, 3, 5, 4)).reshape(2, n, N_CHUNKS, LANES, h)
    return pl.pallas_call(
        _ssm_param_kernel,
        grid=(n,),
        in_specs=[
            pl.BlockSpec((3, 1, N_CHUNKS, LANES), lambda i: (0, i, 0, 0)),
            pl.BlockSpec((3, 1, LANES, N_CHUNKS), lambda i: (0, i, 0, 0)),
            pl.BlockSpec((2, 1, N_CHUNKS, h, LANES), lambda i: (0, i, 0, 0, 0)),
            pl.BlockSpec((2, 1, N_CHUNKS, LANES, h), lambda i: (0, i, 0, 0, 0)),
        ],
        out_specs=(
            pl.BlockSpec((1, N_CHUNKS, 2 * LANES, 2 * LANES), lambda i: (i, 0, 0, 0)),
            pl.BlockSpec((1, N_COLS, COL_K, 2 * LANES), lambda i: (i, 0, 0, 0)),
            pl.BlockSpec((1, N_CHUNKS, 2, SUBLANES, LANES), lambda i: (i, 0, 0, 0, 0)),
        ),
        out_shape=(
            jax.ShapeDtypeStruct((n, N_CHUNKS, 2 * LANES, 2 * LANES), jnp.bfloat16),
            jax.ShapeDtypeStruct((n, N_COLS, COL_K, 2 * LANES), jnp.bfloat16),
            jax.ShapeDtypeStruct((n, N_CHUNKS, 2, SUBLANES, LANES), f32),
        ),
        compiler_params=pltpu.CompilerParams(
            dimension_semantics=("arbitrary",), vmem_limit_bytes=VMEM_LIMIT_BYTES),
        name="ssm_params",
    )(a_lanes, a_rows, b_t, c_t)


def _s5_kernel(x_ref, mods_ref, gain_ref, wz_ref, wy_ref, a2_ref, d_ref, wout_hbm,
               o_ref, z_ref, xs_ref, state_ref, oprev_ref, wout_ref, wout_stage, wout_sems,
               *fetch_scratch, pairs, n_blocks, w_layer):
    bf16 = jnp.bfloat16

    @pl.when(pl.program_id(0) == 0)
    def _():
        state_ref[...] = jnp.zeros_like(state_ref)
        oprev_ref[...] = jnp.zeros_like(oprev_ref)
        _stage_bf16(wout_hbm, w_layer, wout_ref, wout_stage, wout_sems)

    if fetch_scratch:
        x = _fetch_time_major(x_ref, *fetch_scratch, n_blocks)
    else:
        x = x_ref[...]
    mods = mods_ref[0]
    shift, scale, gate = (mods[:, k * D_MODEL:(k + 1) * D_MODEL] for k in range(3))

    half_rows = pairs * BATCH
    x_pairs = x.reshape(pairs, 2, BATCH, D_MODEL)
    x_even = x_pairs[:, 0].reshape(half_rows, D_MODEL)
    x_odd = x_pairs[:, 1].reshape(half_rows, D_MODEL)
    h_even = _norm_modulate(x_even, gain_ref[0], shift, scale)
    h_odd = _norm_modulate(x_odd, gain_ref[0], shift, scale)
    h_odd_prev = jnp.concatenate([oprev_ref[...], h_odd[:half_rows - BATCH]], axis=0)
    oprev_ref[...] = h_odd[half_rows - BATCH:]
    hb_even, hb_odd, hb_odd_prev = (t.astype(bf16) for t in (h_even, h_odd, h_odd_prev))

    ys_even, ys_odd = [], []
    for v in range(N_COLS):
        lanes = slice(v * LANES, (v + 1) * LANES)
        chunks = range(v * CHUNKS_PER_COL, (v + 1) * CHUNKS_PER_COL)
        lhs = jnp.concatenate([hb_odd_prev[:, lanes], hb_even[:, lanes]], axis=-1)
        for j in chunks:
            z_ref[j] = jnp.dot(lhs, wz_ref[0, j], preferred_element_type=jnp.float32)

        carry = [(state_ref[j, 0], state_ref[j, 1]) for j in chunks]
        for m0 in range(0, pairs, BF16_ROWS // SUBLANES):
            for c, j in enumerate(chunks):
                sr, si = carry[c]
                a_re, a_im = a2_ref[0, j, 0], a2_ref[0, j, 1]
                outs_r, outs_i = [], []
                for m in range(m0, m0 + BF16_ROWS // SUBLANES):
                    rows = pl.ds(m * SUBLANES, SUBLANES)
                    z_r = z_ref[j, rows, pl.ds(0, LANES)]
                    z_i = z_ref[j, rows, pl.ds(LANES, LANES)]
                    sr, si = (a_re * sr - a_im * si + z_r, a_re * si + a_im * sr + z_i)
                    outs_r.append(sr)
                    outs_i.append(si)
                carry[c] = (sr, si)
                rows = pl.ds(m0 * SUBLANES, BF16_ROWS)
                col = c * 2 * LANES
                xs_ref[v, rows, pl.ds(col, LANES)] = jnp.concatenate(outs_r, axis=0).astype(bf16)
                xs_ref[v, rows, pl.ds(col + LANES, LANES)] = (
                    jnp.concatenate(outs_i, axis=0).astype(bf16))
        for c, j in enumerate(chunks):
            state_ref[j, 0] = carry[c][0]
            state_ref[j, 1] = carry[c][1]

        lhs = jnp.concatenate([xs_ref[v], hb_odd[:, lanes]], axis=-1)
        y = jnp.dot(lhs, wy_ref[0, v], preferred_element_type=jnp.float32)
        ys_even.append(y[:, :LANES])
        ys_odd.append(y[:, LANES:])

    y = jnp.concatenate([jnp.concatenate(ys_even, axis=-1) + d_ref[0] * h_even,
                         jnp.concatenate(ys_odd, axis=-1) + d_ref[0] * h_odd], axis=0)
    y = jax.nn.gelu(y).astype(bf16)
    z = jnp.dot(y, wout_ref[...], preferred_element_type=jnp.float32)
    mix = z[:, :D_MODEL] * jax.nn.sigmoid(z[:, D_MODEL:])
    out = _gated_residual(jnp.concatenate([x_even, x_odd], axis=0), gate, mix)
    out_pairs = jnp.concatenate([out[:half_rows].reshape(pairs, 1, BATCH, D_MODEL),
                                 out[half_rows:].reshape(pairs, 1, BATCH, D_MODEL)], axis=1)
    o_ref[...] = out_pairs.reshape(2 * half_rows, D_MODEL)


def _s5_layer(x_in, mods, layer, j, gain, wz, wy, a2, d, w_out):
    steps = ROWS_S5 // BATCH
    pairs = steps // 2
    batch_major = x_in.ndim == 3
    rows = x_in.shape[0] * x_in.shape[1] if batch_major else x_in.shape[0]
    n_blocks = rows // ROWS_S5
    if batch_major:
        x_spec = pl.BlockSpec(memory_space=pl.ANY)
        fetch_scratch = [pltpu.VMEM((2, steps, BATCH, D_MODEL), jnp.float32),
                         pltpu.SemaphoreType.DMA((2, BATCH))]
    else:
        x_spec = pl.BlockSpec((ROWS_S5, D_MODEL), lambda i: (i, 0))
        fetch_scratch = []
    return pl.pallas_call(
        functools.partial(_s5_kernel, pairs=pairs, n_blocks=n_blocks, w_layer=j),
        grid=(n_blocks,),
        in_specs=[
            x_spec,
            pl.BlockSpec((1, BATCH, 6 * D_MODEL), lambda i: (layer, 0, 0)),
            pl.BlockSpec((1, 1, D_MODEL), lambda i: (layer, 0, 0)),
            _layer_spec(wz.shape, j),
            _layer_spec(wy.shape, j),
            _layer_spec(a2.shape, j),
            _layer_spec(d.shape, j),
            pl.BlockSpec(memory_space=pl.ANY),
        ],
        out_specs=pl.BlockSpec((ROWS_S5, D_MODEL), lambda i: (i, 0)),
        out_shape=jax.ShapeDtypeStruct((rows, D_MODEL), x_in.dtype),
        scratch_shapes=[
            pltpu.VMEM((N_CHUNKS, pairs * BATCH, 2 * LANES), jnp.float32),
            pltpu.VMEM((N_COLS, pairs * BATCH, CHUNKS_PER_COL * 2 * LANES), jnp.bfloat16),
            pltpu.VMEM((N_CHUNKS, 2, SUBLANES, LANES), jnp.float32),
            pltpu.VMEM((BATCH, D_MODEL), jnp.float32),
        ] + _stage_scratch(w_out.shape) + fetch_scratch,
        compiler_params=pltpu.CompilerParams(
            dimension_semantics=("arbitrary",), vmem_limit_bytes=VMEM_LIMIT_BYTES),
        name="s5_layer",
    )(x_in, mods, gain, wz, wy, a2, d, w_out)


def _conv_kernel(x_ref, mods_ref, gain_ref, win_hbm, cw_ref, wout_hbm, o_ref, cv_ref,
                 win_ref, win_stage, win_sems, wout_ref, wout_stage, wout_sems, *, w_layer):
    rows = x_ref.shape[0]
    hist = (CONV_WIDTH - 1) * BATCH

    @pl.when(pl.program_id(0) == 0)
    def _():
        cv_ref[pl.ds(0, hist), :] = jnp.zeros((hist, D_MODEL), jnp.float32)
        _stage_bf16(win_hbm, w_layer, win_ref, win_stage, win_sems)
        _stage_bf16(wout_hbm, w_layer, wout_ref, wout_stage, wout_sems)

    x = x_ref[...]
    mods = mods_ref[0]
    shift, scale, gate = (mods[:, k * D_MODEL:(k + 1) * D_MODEL] for k in range(3))
    h = _norm_modulate(x, gain_ref[0], shift, scale)
    proj = jnp.dot(h.astype(jnp.bfloat16), win_ref[...], preferred_element_type=jnp.float32)
    bg = proj[:, :D_MODEL]
    cv = proj[:, D_MODEL:2 * D_MODEL] * proj[:, 2 * D_MODEL:]
    cv_ref[pl.ds(hist, rows), :] = cv
    cw = cw_ref[0]
    conv = (cw[2:3] * cv + cw[1:2] * cv_ref[pl.ds(BATCH, rows), :]
            + cw[0:1] * cv_ref[pl.ds(0, rows), :])
    cv_ref[pl.ds(0, hist), :] = cv_ref[pl.ds(rows, hist), :]
    out = jnp.dot((bg * conv).astype(jnp.bfloat16), wout_ref[...],
                  preferred_element_type=jnp.float32)
    o_ref[...] = _gated_residual(x, gate, out)


def _conv_layer(x2, mods, layer, j, gain, w_in, conv_w, w_out):
    rows = x2.shape[0]
    hist = (CONV_WIDTH - 1) * BATCH
    return pl.pallas_call(
        functools.partial(_conv_kernel, w_layer=j),
        grid=(rows // ROWS_CONV,),
        in_specs=[
            pl.BlockSpec((ROWS_CONV, D_MODEL), lambda i: (i, 0)),
            pl.BlockSpec((1, BATCH, 6 * D_MODEL), lambda i: (layer, 0, 0)),
            pl.BlockSpec((1, 1, D_MODEL), lambda i: (layer, 0, 0)),
            pl.BlockSpec(memory_space=pl.ANY),
            _layer_spec(conv_w.shape, j),
            pl.BlockSpec(memory_space=pl.ANY),
        ],
        out_specs=pl.BlockSpec((ROWS_CONV, D_MODEL), lambda i: (i, 0)),
        out_shape=jax.ShapeDtypeStruct(x2.shape, x2.dtype),
        scratch_shapes=([pltpu.VMEM((ROWS_CONV + hist, D_MODEL), jnp.float32)]
                        + _stage_scratch(w_in.shape) + _stage_scratch(w_out.shape)),
        compiler_params=pltpu.CompilerParams(
            dimension_semantics=("arbitrary",), vmem_limit_bytes=VMEM_LIMIT_BYTES),
        name="conv_layer",
    )(x2, mods, gain, w_in, conv_w, w_out)


def _ffn_kernel(x_ref, mods_ref, gain_ref, win_hbm, wout_hbm, fg_ref, o_ref,
                win_ref, win_stage, win_sems, wout_ref, wout_stage, wout_sems, *store_scratch,
                final_norm, n_blocks, w_layer):
    hidden = wout_ref.shape[0]

    @pl.when(pl.program_id(0) == 0)
    def _():
        _stage_bf16(win_hbm, w_layer, win_ref, win_stage, win_sems)
        _stage_bf16(wout_hbm, w_layer, wout_ref, wout_stage, wout_sems)

    mods = mods_ref[0]
    shift, scale, gate = (mods[:, k * D_MODEL:(k + 1) * D_MODEL] for k in range(3, 6))
    half_rows = x_ref.shape[0] // 2
    halves = []
    for half in range(2):
        rows = pl.ds(half * half_rows, half_rows)
        x = x_ref[rows, :]
        h = _norm_modulate(x, gain_ref[0], shift, scale)
        gu = jnp.dot(h.astype(jnp.bfloat16), win_ref[...], preferred_element_type=jnp.float32)
        act = (jax.nn.silu(gu[:, :hidden]) * gu[:, hidden:]).astype(jnp.bfloat16)
        out = jnp.dot(act, wout_ref[...], preferred_element_type=jnp.float32)
        xn = _gated_residual(x, gate, out)
        if final_norm:
            xn = xn * lax.rsqrt(jnp.mean(xn * xn, axis=-1, keepdims=True) + RMS_EPS) * fg_ref[...]
        if store_scratch:
            halves.append(xn)
        else:
            o_ref[rows, :] = xn
    if store_scratch:
        _store_batch_major(jnp.concatenate(halves, axis=0), o_ref, *store_scratch, n_blocks)


def _ffn_layer(x2, mods, layer, gain, w_in, w_out, final_g, final_norm, batch_major_out):
    rows = x2.shape[0]
    n_blocks = rows // ROWS_FFN
    if batch_major_out:
        steps = ROWS_FFN // BATCH
        out_spec = pl.BlockSpec(memory_space=pl.ANY)
        out_shape = jax.ShapeDtypeStruct((BATCH, rows // BATCH, D_MODEL), x2.dtype)
        store_scratch = [pltpu.VMEM((2, steps, BATCH, D_MODEL), jnp.float32),
                         pltpu.SemaphoreType.DMA((2, BATCH))]
    else:
        out_spec = pl.BlockSpec((ROWS_FFN, D_MODEL), lambda i: (i, 0))
        out_shape = jax.ShapeDtypeStruct(x2.shape, x2.dtype)
        store_scratch = []
    return pl.pallas_call(
        functools.partial(_ffn_kernel, final_norm=final_norm, n_blocks=n_blocks, w_layer=layer),
        grid=(n_blocks,),
        in_specs=[
            pl.BlockSpec((ROWS_FFN, D_MODEL), lambda i: (i, 0)),
            pl.BlockSpec((1, BATCH, 6 * D_MODEL), lambda i: (layer, 0, 0)),
            pl.BlockSpec((1, 1, D_MODEL), lambda i: (layer, 0, 0)),
            pl.BlockSpec(memory_space=pl.ANY),
            pl.BlockSpec(memory_space=pl.ANY),
            _const_spec((1, D_MODEL)),
        ],
        out_specs=out_spec,
        out_shape=out_shape,
        scratch_shapes=_stage_scratch(w_in.shape) + _stage_scratch(w_out.shape) + store_scratch,
        compiler_params=pltpu.CompilerParams(
            dimension_semantics=("arbitrary",), vmem_limit_bytes=VMEM_LIMIT_BYTES),
        name="ffn_layer",
    )(x2, mods, gain, w_in, w_out, final_g.reshape(1, D_MODEL))


def kernel(x, c, norm1_g, norm2_g, w_ada, b_ada, ssm_a_re, ssm_a_im, ssm_log_step, ssm_b_re, ssm_b_im, ssm_c_re, ssm_c_im, ssm_d, ssm_w_out, conv_w_in, conv_w, conv_w_out, w_ffn_in, w_ffn_out, final_g):
    mods = _ada_mods(c, w_ada, b_ada)
    gain1 = norm1_g.reshape(DEPTH, 1, D_MODEL)
    gain2 = norm2_g.reshape(DEPTH, 1, D_MODEL)
    wz, wy, a2 = _ssm_prep(ssm_a_re, ssm_a_im, ssm_log_step, ssm_b_re, ssm_b_im,
                          ssm_c_re, ssm_c_im)
    ssm_d3 = ssm_d.reshape(-1, 1, D_MODEL)
    x2 = x
    for i in range(DEPTH):
        j = i // N_MIXERS
        if i % N_MIXERS == 0:
            x2 = _s5_layer(x2, mods, i, j, gain1, wz, wy, a2, ssm_d3, ssm_w_out)
        else:
            x2 = _conv_layer(x2, mods, i, j, gain1, conv_w_in, conv_w, conv_w_out)
        last = i == DEPTH - 1
        x2 = _ffn_layer(x2, mods, i, gain2, w_ffn_in, w_ffn_out, final_g,
                        final_norm=last, batch_major_out=last)
    return x2
```

```python
import functools

import jax
import jax.numpy as jnp
from jax import lax
from jax.experimental import pallas as pl
from jax.experimental.pallas import tpu as pltpu

D_MODEL = 1024
BATCH = 8
DEPTH = 4
N_MIXERS = 2
SSM_GROUP = 16
SSM_GROUPS = D_MODEL // SSM_GROUP
SSM_STATE = 64
CONV_WIDTH = 3
RMS_EPS = 1e-6

LANES = 128
SUBLANES = 8
BF16_ROWS = 16
VMEM_LIMIT_BYTES = 56 * 1024 * 1024
STAGE_BYTES = 3 * 512 * 1024

N_CHUNKS = SSM_GROUPS * SSM_STATE // LANES
GROUPS_PER_CHUNK = LANES // SSM_STATE
N_COLS = D_MODEL // LANES
GROUPS_PER_COL = LANES // SSM_GROUP
CHUNKS_PER_COL = N_CHUNKS // N_COLS
COL_K = CHUNKS_PER_COL * 2 * LANES + LANES

ROWS_FFN = 1024
FFN_SUB_BLOCKS = 4
ROWS_CONV = 1024
ROWS_S5 = 512
ADA_COLS = 3072


def _const_spec(shape, index=None):
    index = (0,) * len(shape) if index is None else index
    return pl.BlockSpec(shape, lambda *_: index, pipeline_mode=pl.Buffered(1))


def _layer_spec(shape, layer):
    return _const_spec((1,) + tuple(shape[1:]), (layer,) + (0,) * (len(shape) - 1))


def _norm_modulate(x, gain, shift, scale):
    rows = x.shape[0]
    y = x * lax.rsqrt(jnp.mean(x * x, axis=-1, keepdims=True) + RMS_EPS)
    y = y.reshape(rows // BATCH, BATCH, D_MODEL)
    return (y * (gain * (1.0 + scale))[None] + shift[None]).reshape(rows, D_MODEL)


def _gated_residual(x, gate, update):
    rows = x.shape[0]
    upd = update.reshape(rows // BATCH, BATCH, D_MODEL) * gate[None]
    return x + upd.reshape(rows, D_MODEL)


def _stage_scratch(shape):
    k_dim, n_dim = shape[-2:]
    fits = [r for r in range(BF16_ROWS, k_dim + 1, BF16_ROWS)
            if k_dim % r == 0 and r * n_dim * 4 <= STAGE_BYTES]
    return [pltpu.VMEM((k_dim, n_dim), jnp.bfloat16),
            pltpu.VMEM((2, max(fits), n_dim), jnp.float32),
            pltpu.SemaphoreType.DMA((2,))]


def _stage_bf16(w_hbm, layer, w_bf16, stage, sems):
    chunk = stage.shape[1]
    n_chunks = w_bf16.shape[0] // chunk

    def copy(k):
        return pltpu.make_async_copy(w_hbm.at[layer, pl.ds(k * chunk, chunk), :],
                                     stage.at[k % 2], sems.at[k % 2])

    copy(0).start()
    for k in range(n_chunks):
        if k + 1 < n_chunks:
            copy(k + 1).start()
        copy(k).wait()
        w_bf16[pl.ds(k * chunk, chunk), :] = stage[k % 2].astype(jnp.bfloat16)


def _batch_row_copies(hbm, buf, sems, block, slot, to_hbm):
    steps = buf.shape[1]
    copies = []
    for b in range(BATCH):
        hbm_rows = hbm.at[b, pl.ds(block * steps, steps), :]
        vmem_rows = buf.at[slot, :, b, :]
        src, dst = (vmem_rows, hbm_rows) if to_hbm else (hbm_rows, vmem_rows)
        copies.append(pltpu.make_async_copy(src, dst, sems.at[slot, b]))
    return copies


def _fetch_time_major(x_hbm, buf, sems, n_blocks):
    i = pl.program_id(0)

    @pl.when(i == 0)
    def _():
        for c in _batch_row_copies(x_hbm, buf, sems, 0, 0, to_hbm=False):
            c.start()

    @pl.when(i + 1 < n_blocks)
    def _():
        for c in _batch_row_copies(x_hbm, buf, sems, i + 1, (i + 1) % 2, to_hbm=False):
            c.start()

    for c in _batch_row_copies(x_hbm, buf, sems, i, i % 2, to_hbm=False):
        c.wait()
    return buf[i % 2].reshape(buf.shape[1] * BATCH, D_MODEL)


def _store_batch_major(value, o_hbm, buf, sems, n_blocks):
    i = pl.program_id(0)
    slot = i % 2

    @pl.when(i >= 2)
    def _():
        for c in _batch_row_copies(o_hbm, buf, sems, i - 2, slot, to_hbm=True):
            c.wait()

    buf[slot] = value.reshape(buf.shape[1], BATCH, D_MODEL)
    for c in _batch_row_copies(o_hbm, buf, sems, i, slot, to_hbm=True):
        c.start()

    @pl.when(i == n_blocks - 1)
    def _():
        for back in range(min(2, n_blocks)):
            for c in _batch_row_copies(o_hbm, buf, sems, i - back, (i - back) % 2, to_hbm=True):
                c.wait()


def _ada_kernel(c_ref, w_ref, b_ref, o_ref):
    c_act = jax.nn.silu(c_ref[...])
    o_ref[0] = jnp.dot(c_act, w_ref[0], preferred_element_type=jnp.float32) + b_ref[0]


def _ada_mods(c, w_ada, b_ada):
    n_cols = 6 * D_MODEL
    return pl.pallas_call(
        _ada_kernel,
        grid=(DEPTH, n_cols // ADA_COLS),
        in_specs=[
            pl.BlockSpec((BATCH, D_MODEL), lambda i, j: (0, 0)),
            pl.BlockSpec((1, D_MODEL, ADA_COLS), lambda i, j: (i, 0, j)),
            pl.BlockSpec((1, 1, ADA_COLS), lambda i, j: (i, 0, j)),
        ],
        out_specs=pl.BlockSpec((1, BATCH, ADA_COLS), lambda i, j: (i, 0, j)),
        out_shape=jax.ShapeDtypeStruct((DEPTH, BATCH, n_cols), jnp.float32),
        compiler_params=pltpu.CompilerParams(
            dimension_semantics=("arbitrary", "arbitrary"),
            vmem_limit_bytes=VMEM_LIMIT_BYTES),
        name="ada_mods",
    )(c, w_ada, b_ada.reshape(DEPTH, 1, n_cols))


def _zoh(a_re, a_im, log_step):
    lr = jnp.minimum(a_re, -1e-4)
    li = a_im
    dt = jnp.exp(log_step)
    mag = jnp.exp(lr * dt)
    abr = mag * jnp.cos(li * dt)
    abi = mag * jnp.sin(li * dt)
    den = lr * lr + li * li
    qr = ((abr - 1.0) * lr + abi * li) / den
    qi = (abi * lr - (abr - 1.0) * li) / den
    return abr, abi, qr, qi


def _ssm_param_kernel(al_ref, at_ref, bt_ref, ct_ref, wz_ref, wy_ref, a2_ref):
    bf16 = jnp.bfloat16
    abr, abi, qr, qi = _zoh(al_ref[0, 0], al_ref[1, 0], al_ref[2, 0])
    a2r = abr * abr - abi * abi
    a2i = 2.0 * abr * abi
    a2_ref[0, :, 0] = jnp.broadcast_to(a2r[:, None, :], (N_CHUNKS, SUBLANES, LANES))
    a2_ref[0, :, 1] = jnp.broadcast_to(a2i[:, None, :], (N_CHUNKS, SUBLANES, LANES))
    btr = bt_ref[0, 0]
    bti = bt_ref[1, 0]
    bbr = qr[:, None, :] * btr - qi[:, None, :] * bti
    bbi = qr[:, None, :] * bti + qi[:, None, :] * btr
    bar = abr[:, None, :] * bbr - abi[:, None, :] * bbi
    bai = abr[:, None, :] * bbi + abi[:, None, :] * bbr
    abr_t, abi_t, _, _ = _zoh(at_ref[0, 0], at_ref[1, 0], at_ref[2, 0])
    ctr = ct_ref[0, 0]
    cti = ct_ref[1, 0]

    row_group = lax.broadcasted_iota(jnp.int32, (LANES, LANES), 0) // SSM_GROUP
    col_group = lax.broadcasted_iota(jnp.int32, (LANES, LANES), 1) // SSM_GROUP
    row_sub = lax.broadcasted_iota(jnp.int32, (LANES, LANES), 0) // SSM_STATE
    col_sub = lax.broadcasted_iota(jnp.int32, (LANES, LANES), 1) // SSM_STATE
    lane_sub = lax.broadcasted_iota(jnp.int32, (SSM_GROUP, LANES), 1) // SSM_STATE
    spread = (lax.broadcasted_iota(jnp.int32, (SSM_GROUP, LANES), 1) % SSM_GROUP
              == lax.broadcasted_iota(jnp.int32, (SSM_GROUP, LANES), 0)).astype(bf16)

    def rows_by_group(block):
        rep = jnp.broadcast_to(block[None], (GROUPS_PER_COL, SSM_GROUP, block.shape[-1]))
        return rep.reshape(LANES, block.shape[-1])

    def cols_by_group(blocks):
        flat = blocks.reshape(-1, SSM_GROUP).astype(bf16)
        return jnp.dot(flat, spread, preferred_element_type=jnp.float32)

    for j in range(N_CHUNKS):
        c = j % CHUNKS_PER_COL
        b_live = row_group == c * GROUPS_PER_CHUNK + col_sub
        for half, (re, im) in enumerate(((bar, bai), (bbr, bbi))):
            rows = pl.ds(half * LANES, LANES)
            wz_ref[0, j, rows, pl.ds(0, LANES)] = (
                jnp.where(b_live, rows_by_group(re[j]), 0.0).astype(bf16))
            wz_ref[0, j, rows, pl.ds(LANES, LANES)] = (
                jnp.where(b_live, rows_by_group(im[j]), 0.0).astype(bf16))

    car, cai = [], []
    for j in range(N_CHUNKS):
        a_r, a_i = abr_t[:, j:j + 1], abi_t[:, j:j + 1]
        car.append(ctr[j] * a_r - cti[j] * a_i)
        cai.append(ctr[j] * a_i + cti[j] * a_r)
    car, cai = jnp.stack(car), jnp.stack(cai)
    spread_c = [(cols_by_group(ctr), cols_by_group(car)), (cols_by_group(-cti), cols_by_group(-cai))]
    for j in range(N_CHUNKS):
        v, c = divmod(j, CHUNKS_PER_COL)
        c_live = col_group == c * GROUPS_PER_CHUNK + row_sub
        for part, (even, odd) in enumerate(spread_c):
            rows = pl.ds(c * 2 * LANES + part * LANES, LANES)
            src_rows = slice(j * LANES, (j + 1) * LANES)
            wy_ref[0, v, rows, pl.ds(0, LANES)] = jnp.where(c_live, even[src_rows], 0.0).astype(bf16)
            wy_ref[0, v, rows, pl.ds(LANES, LANES)] = jnp.where(c_live, odd[src_rows], 0.0).astype(bf16)

    cb = []
    for l in range(GROUPS_PER_CHUNK):
        mine = (lane_sub == l)[None]
        cb.append(jnp.einsum('jhk,jko->jho', jnp.where(mine, bbr, 0.0), ctr,
                             preferred_element_type=jnp.float32)
                  - jnp.einsum('jhk,jko->jho', jnp.where(mine, bbi, 0.0), cti,
                               preferred_element_type=jnp.float32))
    cb = cols_by_group(jnp.stack(cb, axis=1))
    on_diag = row_group == col_group
    for v in range(N_COLS):
        rows = pl.ds(CHUNKS_PER_COL * 2 * LANES, LANES)
        wy_ref[0, v, rows, pl.ds(0, LANES)] = jnp.zeros((LANES, LANES), bf16)
        wy_ref[0, v, rows, pl.ds(LANES, LANES)] = (
            jnp.where(on_diag, cb[v * LANES:(v + 1) * LANES], 0.0).astype(bf16))


def _ssm_prep(a_re, a_im, log_step, b_re, b_im, c_re, c_im):
    n = a_re.shape[0]
    f32 = jnp.float32
    p, h = SSM_STATE, SSM_GROUP
    a_lanes = jnp.stack([a_re, a_im, jnp.broadcast_to(log_step[:, :, None], a_re.shape)])
    a_lanes = a_lanes.reshape(3, n, N_CHUNKS, LANES)
    a_rows = jnp.swapaxes(a_lanes, 2, 3)
    b_t = jnp.stack([b_re, b_im]).reshape(2, n, N_CHUNKS, GROUPS_PER_CHUNK, p, h)
    b_t = jnp.transpose(b_t, (0, 1, 2, 5, 3, 4)).reshape(2, n, N_CHUNKS, h, LANES)
    c_t = jnp.stack([c_re, c_im]).reshape(2, n, N_CHUNKS, GROUPS_PER_CHUNK, h, p)
    c_t = jnp.transpose(c_t, (0, 1, 2, 3, 5, 4)).reshape(2, n, N_CHUNKS, LANES, h)
    return pl.pallas_call(
        _ssm_param_kernel,
        grid=(n,),
        in_specs=[
            pl.BlockSpec((3, 1, N_CHUNKS, LANES), lambda i: (0, i, 0, 0)),
            pl.BlockSpec((3, 1, LANES, N_CHUNKS), lambda i: (0, i, 0, 0)),
            pl.BlockSpec((2, 1, N_CHUNKS, h, LANES), lambda i: (0, i, 0, 0, 0)),
            pl.BlockSpec((2, 1, N_CHUNKS, LANES, h), lambda i: (0, i, 0, 0, 0)),
        ],
        out_specs=(
            pl.BlockSpec((1, N_CHUNKS, 2 * LANES, 2 * LANES), lambda i: (i, 0, 0, 0)),
            pl.BlockSpec((1, N_COLS, COL_K, 2 * LANES), lambda i: (i, 0, 0, 0)),
            pl.BlockSpec((1, N_CHUNKS, 2, SUBLANES, LANES), lambda i: (i, 0, 0, 0, 0)),
        ),
        out_shape=(
            jax.ShapeDtypeStruct((n, N_CHUNKS, 2 * LANES, 2 * LANES), jnp.bfloat16),
            jax.ShapeDtypeStruct((n, N_COLS, COL_K, 2 * LANES), jnp.bfloat16),
            jax.ShapeDtypeStruct((n, N_CHUNKS, 2, SUBLANES, LANES), f32),
        ),
        compiler_params=pltpu.CompilerParams(
            dimension_semantics=("arbitrary",), vmem_limit_bytes=VMEM_LIMIT_BYTES),
        name="ssm_params",
    )(a_lanes, a_rows, b_t, c_t)


def _s5_kernel(x_ref, mods_ref, gain_ref, wz_ref, wy_ref, a2_ref, d_ref, wout_hbm,
               o_ref, z_ref, xs_ref, state_ref, oprev_ref, wout_ref, wout_stage, wout_sems,
               *fetch_scratch, pairs, n_blocks, w_layer):
    bf16 = jnp.bfloat16

    @pl.when(pl.program_id(0) == 0)
    def _():
        state_ref[...] = jnp.zeros_like(state_ref)
        oprev_ref[...] = jnp.zeros_like(oprev_ref)
        _stage_bf16(wout_hbm, w_layer, wout_ref, wout_stage, wout_sems)

    if fetch_scratch:
        x = _fetch_time_major(x_ref, *fetch_scratch, n_blocks)
    else:
        x = x_ref[...]
    mods = mods_ref[0]
    shift, scale, gate = (mods[:, k * D_MODEL:(k + 1) * D_MODEL] for k in range(3))

    half_rows = pairs * BATCH
    x_pairs = x.reshape(pairs, 2, BATCH, D_MODEL)
    x_even = x_pairs[:, 0].reshape(half_rows, D_MODEL)
    x_odd = x_pairs[:, 1].reshape(half_rows, D_MODEL)
    h_even = _norm_modulate(x_even, gain_ref[0], shift, scale)
    h_odd = _norm_modulate(x_odd, gain_ref[0], shift, scale)
    h_odd_prev = jnp.concatenate([oprev_ref[...], h_odd[:half_rows - BATCH]], axis=0)
    oprev_ref[...] = h_odd[half_rows - BATCH:]
    hb_even, hb_odd, hb_odd_prev = (t.astype(bf16) for t in (h_even, h_odd, h_odd_prev))

    ys_even, ys_odd = [], []
    for v in range(N_COLS):
        lanes = slice(v * LANES, (v + 1) * LANES)
        chunks = range(v * CHUNKS_PER_COL, (v + 1) * CHUNKS_PER_COL)
        lhs = jnp.concatenate([hb_odd_prev[:, lanes], hb_even[:, lanes]], axis=-1)
        for j in chunks:
            z_ref[j] = jnp.dot(lhs, wz_ref[0, j], preferred_element_type=jnp.float32)

        carry = [(state_ref[j, 0], state_ref[j, 1]) for j in chunks]
        for m0 in range(0, pairs, BF16_ROWS // SUBLANES):
            for c, j in enumerate(chunks):
                sr, si = carry[c]
                a_re, a_im = a2_ref[0, j, 0], a2_ref[0, j, 1]
                outs_r, outs_i = [], []
                for m in range(m0, m0 + BF16_ROWS // SUBLANES):
                    rows = pl.ds(m * SUBLANES, SUBLANES)
                    z_r = z_ref[j, rows, pl.ds(0, LANES)]
                    z_i = z_ref[j, rows, pl.ds(LANES, LANES)]
                    sr, si = (a_re * sr - a_im * si + z_r, a_re * si + a_im * sr + z_i)
                    outs_r.append(sr)
                    outs_i.append(si)
                carry[c] = (sr, si)
                rows = pl.ds(m0 * SUBLANES, BF16_ROWS)
                col = c * 2 * LANES
                xs_ref[v, rows, pl.ds(col, LANES)] = jnp.concatenate(outs_r, axis=0).astype(bf16)
                xs_ref[v, rows, pl.ds(col + LANES, LANES)] = (
                    jnp.concatenate(outs_i, axis=0).astype(bf16))
        for c, j in enumerate(chunks):
            state_ref[j, 0] = carry[c][0]
            state_ref[j, 1] = carry[c][1]

        lhs = jnp.concatenate([xs_ref[v], hb_odd[:, lanes]], axis=-1)
        y = jnp.dot(lhs, wy_ref[0, v], preferred_element_type=jnp.float32)
        ys_even.append(y[:, :LANES])
        ys_odd.append(y[:, LANES:])

    y = jnp.concatenate([jnp.concatenate(ys_even, axis=-1) + d_ref[0] * h_even,
                         jnp.concatenate(ys_odd, axis=-1) + d_ref[0] * h_odd], axis=0)
    y = jax.nn.gelu(y).astype(bf16)
    z = jnp.dot(y, wout_ref[...], preferred_element_type=jnp.float32)
    mix = z[:, :D_MODEL] * jax.nn.sigmoid(z[:, D_MODEL:])
    out = _gated_residual(jnp.concatenate([x_even, x_odd], axis=0), gate, mix)
    out_pairs = jnp.concatenate([out[:half_rows].reshape(pairs, 1, BATCH, D_MODEL),
                                 out[half_rows:].reshape(pairs, 1, BATCH, D_MODEL)], axis=1)
    o_ref[...] = out_pairs.reshape(2 * half_rows, D_MODEL)


def _s5_layer(x_in, mods, layer, j, gain, wz, wy, a2, d, w_out):
    steps = ROWS_S5 // BATCH
    pairs = steps // 2
    batch_major = x_in.ndim == 3
    rows = x_in.shape[0] * x_in.shape[1] if batch_major else x_in.shape[0]
    n_blocks = rows // ROWS_S5
    if batch_major:
        x_spec = pl.BlockSpec(memory_space=pl.ANY)
        fetch_scratch = [pltpu.VMEM((2, steps, BATCH, D_MODEL), jnp.float32),
                         pltpu.SemaphoreType.DMA((2, BATCH))]
    else:
        x_spec = pl.BlockSpec((ROWS_S5, D_MODEL), lambda i: (i, 0))
        fetch_scratch = []
    return pl.pallas_call(
        functools.partial(_s5_kernel, pairs=pairs, n_blocks=n_blocks, w_layer=j),
        grid=(n_blocks,),
        in_specs=[
            x_spec,
            pl.BlockSpec((1, BATCH, 6 * D_MODEL), lambda i: (layer, 0, 0)),
            pl.BlockSpec((1, 1, D_MODEL), lambda i: (layer, 0, 0)),
            _layer_spec(wz.shape, j),
            _layer_spec(wy.shape, j),
            _layer_spec(a2.shape, j),
            _layer_spec(d.shape, j),
            pl.BlockSpec(memory_space=pl.ANY),
        ],
        out_specs=pl.BlockSpec((ROWS_S5, D_MODEL), lambda i: (i, 0)),
        out_shape=jax.ShapeDtypeStruct((rows, D_MODEL), x_in.dtype),
        scratch_shapes=[
            pltpu.VMEM((N_CHUNKS, pairs * BATCH, 2 * LANES), jnp.float32),
            pltpu.VMEM((N_COLS, pairs * BATCH, CHUNKS_PER_COL * 2 * LANES), jnp.bfloat16),
            pltpu.VMEM((N_CHUNKS, 2, SUBLANES, LANES), jnp.float32),
            pltpu.VMEM((BATCH, D_MODEL), jnp.float32),
        ] + _stage_scratch(w_out.shape) + fetch_scratch,
        compiler_params=pltpu.CompilerParams(
            dimension_semantics=("arbitrary",), vmem_limit_bytes=VMEM_LIMIT_BYTES),
        name="s5_layer",
    )(x_in, mods, gain, wz, wy, a2, d, w_out)


def _conv_kernel(x_ref, mods_ref, gain_ref, win_hbm, cw_ref, wout_hbm, o_ref, cv_ref,
                 win_ref, win_stage, win_sems, wout_ref, wout_stage, wout_sems, *, w_layer):
    rows = x_ref.shape[0]
    hist = (CONV_WIDTH - 1) * BATCH

    @pl.when(pl.program_id(0) == 0)
    def _():
        cv_ref[pl.ds(0, hist), :] = jnp.zeros((hist, D_MODEL), jnp.float32)
        _stage_bf16(win_hbm, w_layer, win_ref, win_stage, win_sems)
        _stage_bf16(wout_hbm, w_layer, wout_ref, wout_stage, wout_sems)

    x = x_ref[...]
    mods = mods_ref[0]
    shift, scale, gate = (mods[:, k * D_MODEL:(k + 1) * D_MODEL] for k in range(3))
    h = _norm_modulate(x, gain_ref[0], shift, scale)
    proj = jnp.dot(h.astype(jnp.bfloat16), win_ref[...], preferred_element_type=jnp.float32)
    bg = proj[:, :D_MODEL]
    cv = proj[:, D_MODEL:2 * D_MODEL] * proj[:, 2 * D_MODEL:]
    cv_ref[pl.ds(hist, rows), :] = cv
    cw = cw_ref[0]
    conv = (cw[2:3] * cv + cw[1:2] * cv_ref[pl.ds(BATCH, rows), :]
            + cw[0:1] * cv_ref[pl.ds(0, rows), :])
    cv_ref[pl.ds(0, hist), :] = cv_ref[pl.ds(rows, hist), :]
    out = jnp.dot((bg * conv).astype(jnp.bfloat16), wout_ref[...],
                  preferred_element_type=jnp.float32)
    o_ref[...] = _gated_residual(x, gate, out)


def _conv_layer(x2, mods, layer, j, gain, w_in, conv_w, w_out):
    rows = x2.shape[0]
    hist = (CONV_WIDTH - 1) * BATCH
    return pl.pallas_call(
        functools.partial(_conv_kernel, w_layer=j),
        grid=(rows // ROWS_CONV,),
        in_specs=[
            pl.BlockSpec((ROWS_CONV, D_MODEL), lambda i: (i, 0)),
            pl.BlockSpec((1, BATCH, 6 * D_MODEL), lambda i: (layer, 0, 0)),
            pl.BlockSpec((1, 1, D_MODEL), lambda i: (layer, 0, 0)),
            pl.BlockSpec(memory_space=pl.ANY),
            _layer_spec(conv_w.shape, j),
            pl.BlockSpec(memory_space=pl.ANY),
        ],
        out_specs=pl.BlockSpec((ROWS_CONV, D_MODEL), lambda i: (i, 0)),
        out_shape=jax.ShapeDtypeStruct(x2.shape, x2.dtype),
        scratch_shapes=([pltpu.VMEM((ROWS_CONV + hist, D_MODEL), jnp.float32)]
                        + _stage_scratch(w_in.shape) + _stage_scratch(w_out.shape)),
        compiler_params=pltpu.CompilerParams(
            dimension_semantics=("arbitrary",), vmem_limit_bytes=VMEM_LIMIT_BYTES),
        name="conv_layer",
    )(x2, mods, gain, w_in, conv_w, w_out)


def _ffn_kernel(x_ref, mods_ref, gain_ref, win_hbm, wout_hbm, fg_ref, o_ref,
                win_ref, win_stage, win_sems, wout_ref, wout_stage, wout_sems, *store_scratch,
                final_norm, n_blocks, w_layer):
    hidden = wout_ref.shape[0]

    @pl.when(pl.program_id(0) == 0)
    def _():
        _stage_bf16(win_hbm, w_layer, win_ref, win_stage, win_sems)
        _stage_bf16(wout_hbm, w_layer, wout_ref, wout_stage, wout_sems)

    mods = mods_ref[0]
    shift, scale, gate = (mods[:, k * D_MODEL:(k + 1) * D_MODEL] for k in range(3, 6))
    sub_rows = x_ref.shape[0] // FFN_SUB_BLOCKS
    halves = []
    for q in range(FFN_SUB_BLOCKS):
        rows = pl.ds(q * sub_rows, sub_rows)
        x = x_ref[rows, :]
        h = _norm_modulate(x, gain_ref[0], shift, scale)
        gu = jnp.dot(h.astype(jnp.bfloat16), win_ref[...], preferred_element_type=jnp.float32)
        act = (jax.nn.silu(gu[:, :hidden]) * gu[:, hidden:]).astype(jnp.bfloat16)
        out = jnp.dot(act, wout_ref[...], preferred_element_type=jnp.float32)
        xn = _gated_residual(x, gate, out)
        if final_norm:
            xn = xn * lax.rsqrt(jnp.mean(xn * xn, axis=-1, keepdims=True) + RMS_EPS) * fg_ref[...]
        if store_scratch:
            halves.append(xn)
        else:
            o_ref[rows, :] = xn
    if store_scratch:
        _store_batch_major(jnp.concatenate(halves, axis=0), o_ref, *store_scratch, n_blocks)


def _ffn_layer(x2, mods, layer, gain, w_in, w_out, final_g, final_norm, batch_major_out):
    rows = x2.shape[0]
    n_blocks = rows // ROWS_FFN
    if batch_major_out:
        steps = ROWS_FFN // BATCH
        out_spec = pl.BlockSpec(memory_space=pl.ANY)
        out_shape = jax.ShapeDtypeStruct((BATCH, rows // BATCH, D_MODEL), x2.dtype)
        store_scratch = [pltpu.VMEM((2, steps, BATCH, D_MODEL), jnp.float32),
                         pltpu.SemaphoreType.DMA((2, BATCH))]
    else:
        out_spec = pl.BlockSpec((ROWS_FFN, D_MODEL), lambda i: (i, 0))
        out_shape = jax.ShapeDtypeStruct(x2.shape, x2.dtype)
        store_scratch = []
    return pl.pallas_call(
        functools.partial(_ffn_kernel, final_norm=final_norm, n_blocks=n_blocks, w_layer=layer),
        grid=(n_blocks,),
        in_specs=[
            pl.BlockSpec((ROWS_FFN, D_MODEL), lambda i: (i, 0)),
            pl.BlockSpec((1, BATCH, 6 * D_MODEL), lambda i: (layer, 0, 0)),
            pl.BlockSpec((1, 1, D_MODEL), lambda i: (layer, 0, 0)),
            pl.BlockSpec(memory_space=pl.ANY),
            pl.BlockSpec(memory_space=pl.ANY),
            _const_spec((1, D_MODEL)),
        ],
        out_specs=out_spec,
        out_shape=out_shape,
        scratch_shapes=_stage_scratch(w_in.shape) + _stage_scratch(w_out.shape) + store_scratch,
        compiler_params=pltpu.CompilerParams(
            dimension_semantics=("arbitrary",), vmem_limit_bytes=VMEM_LIMIT_BYTES),
        name="ffn_layer",
    )(x2, mods, gain, w_in, w_out, final_g.reshape(1, D_MODEL))


def kernel(x, c, norm1_g, norm2_g, w_ada, b_ada, ssm_a_re, ssm_a_im, ssm_log_step, ssm_b_re, ssm_b_im, ssm_c_re, ssm_c_im, ssm_d, ssm_w_out, conv_w_in, conv_w, conv_w_out, w_ffn_in, w_ffn_out, final_g):
    mods = _ada_mods(c, w_ada, b_ada)
    gain1 = norm1_g.reshape(DEPTH, 1, D_MODEL)
    gain2 = norm2_g.reshape(DEPTH, 1, D_MODEL)
    wz, wy, a2 = _ssm_prep(ssm_a_re, ssm_a_im, ssm_log_step, ssm_b_re, ssm_b_im,
                          ssm_c_re, ssm_c_im)
    ssm_d3 = ssm_d.reshape(-1, 1, D_MODEL)
    x2 = x
    for i in range(DEPTH):
        j = i // N_MIXERS
        if i % N_MIXERS == 0:
            x2 = _s5_layer(x2, mods, i, j, gain1, wz, wy, a2, ssm_d3, ssm_w_out)
        else:
            x2 = _conv_layer(x2, mods, i, j, gain1, conv_w_in, conv_w, conv_w_out)
        last = i == DEPTH - 1
        x2 = _ffn_layer(x2, mods, i, gain2, w_ffn_in, w_ffn_out, final_g,
                        final_norm=last, batch_major_out=last)
    return x2
```

```python
import functools

import jax
import jax.numpy as jnp
from jax import lax
from jax.experimental import pallas as pl
from jax.experimental.pallas import tpu as pltpu

D_MODEL = 1024
BATCH = 8
DEPTH = 4
N_MIXERS = 2
SSM_GROUP = 16
SSM_GROUPS = D_MODEL // SSM_GROUP
SSM_STATE = 64
CONV_WIDTH = 3
RMS_EPS = 1e-6

LANES = 128
SUBLANES = 8
BF16_ROWS = 16
VMEM_LIMIT_BYTES = 56 * 1024 * 1024
STAGE_BYTES = 3 * 512 * 1024

N_CHUNKS = SSM_GROUPS * SSM_STATE // LANES
GROUPS_PER_CHUNK = LANES // SSM_STATE
N_COLS = D_MODEL // LANES
GROUPS_PER_COL = LANES // SSM_GROUP
CHUNKS_PER_COL = N_CHUNKS // N_COLS
COL_K = CHUNKS_PER_COL * 2 * LANES + LANES

ROWS_FFN = 1024
FFN_SUB_BLOCKS = 8
ROWS_CONV = 1024
ROWS_S5 = 512
ADA_COLS = 3072


def _const_spec(shape, index=None):
    index = (0,) * len(shape) if index is None else index
    return pl.BlockSpec(shape, lambda *_: index, pipeline_mode=pl.Buffered(1))


def _layer_spec(shape, layer):
    return _const_spec((1,) + tuple(shape[1:]), (layer,) + (0,) * (len(shape) - 1))


def _norm_modulate(x, gain, shift, scale):
    rows = x.shape[0]
    y = x * lax.rsqrt(jnp.mean(x * x, axis=-1, keepdims=True) + RMS_EPS)
    y = y.reshape(rows // BATCH, BATCH, D_MODEL)
    return (y * (gain * (1.0 + scale))[None] + shift[None]).reshape(rows, D_MODEL)


def _gated_residual(x, gate, update):
    rows = x.shape[0]
    upd = update.reshape(rows // BATCH, BATCH, D_MODEL) * gate[None]
    return x + upd.reshape(rows, D_MODEL)


def _stage_scratch(shape):
    k_dim, n_dim = shape[-2:]
    fits = [r for r in range(BF16_ROWS, k_dim + 1, BF16_ROWS)
            if k_dim % r == 0 and r * n_dim * 4 <= STAGE_BYTES]
    return [pltpu.VMEM((k_dim, n_dim), jnp.bfloat16),
            pltpu.VMEM((2, max(fits), n_dim), jnp.float32),
            pltpu.SemaphoreType.DMA((2,))]


def _stage_bf16(w_hbm, layer, w_bf16, stage, sems):
    chunk = stage.shape[1]
    n_chunks = w_bf16.shape[0] // chunk

    def copy(k):
        return pltpu.make_async_copy(w_hbm.at[layer, pl.ds(k * chunk, chunk), :],
                                     stage.at[k % 2], sems.at[k % 2])

    copy(0).start()
    for k in range(n_chunks):
        if k + 1 < n_chunks:
            copy(k + 1).start()
        copy(k).wait()
        w_bf16[pl.ds(k * chunk, chunk), :] = stage[k % 2].astype(jnp.bfloat16)


def _batch_row_copies(hbm, buf, sems, block, slot, to_hbm):
    steps = buf.shape[1]
    copies = []
    for b in range(BATCH):
        hbm_rows = hbm.at[b, pl.ds(block * steps, steps), :]
        vmem_rows = buf.at[slot, :, b, :]
        src, dst = (vmem_rows, hbm_rows) if to_hbm else (hbm_rows, vmem_rows)
        copies.append(pltpu.make_async_copy(src, dst, sems.at[slot, b]))
    return copies


def _fetch_time_major(x_hbm, buf, sems, n_blocks):
    i = pl.program_id(0)

    @pl.when(i == 0)
    def _():
        for c in _batch_row_copies(x_hbm, buf, sems, 0, 0, to_hbm=False):
            c.start()

    @pl.when(i + 1 < n_blocks)
    def _():
        for c in _batch_row_copies(x_hbm, buf, sems, i + 1, (i + 1) % 2, to_hbm=False):
            c.start()

    for c in _batch_row_copies(x_hbm, buf, sems, i, i % 2, to_hbm=False):
        c.wait()
    return buf[i % 2].reshape(buf.shape[1] * BATCH, D_MODEL)


def _store_batch_major(value, o_hbm, buf, sems, n_blocks):
    i = pl.program_id(0)
    slot = i % 2

    @pl.when(i >= 2)
    def _():
        for c in _batch_row_copies(o_hbm, buf, sems, i - 2, slot, to_hbm=True):
            c.wait()

    buf[slot] = value.reshape(buf.shape[1], BATCH, D_MODEL)
    for c in _batch_row_copies(o_hbm, buf, sems, i, slot, to_hbm=True):
        c.start()

    @pl.when(i == n_blocks - 1)
    def _():
        for back in range(min(2, n_blocks)):
            for c in _batch_row_copies(o_hbm, buf, sems, i - back, (i - back) % 2, to_hbm=True):
                c.wait()


def _ada_kernel(c_ref, w_ref, b_ref, o_ref):
    c_act = jax.nn.silu(c_ref[...])
    o_ref[0] = jnp.dot(c_act, w_ref[0], preferred_element_type=jnp.float32) + b_ref[0]


def _ada_mods(c, w_ada, b_ada):
    n_cols = 6 * D_MODEL
    return pl.pallas_call(
        _ada_kernel,
        grid=(DEPTH, n_cols // ADA_COLS),
        in_specs=[
            pl.BlockSpec((BATCH, D_MODEL), lambda i, j: (0, 0)),
            pl.BlockSpec((1, D_MODEL, ADA_COLS), lambda i, j: (i, 0, j)),
            pl.BlockSpec((1, 1, ADA_COLS), lambda i, j: (i, 0, j)),
        ],
        out_specs=pl.BlockSpec((1, BATCH, ADA_COLS), lambda i, j: (i, 0, j)),
        out_shape=jax.ShapeDtypeStruct((DEPTH, BATCH, n_cols), jnp.float32),
        compiler_params=pltpu.CompilerParams(
            dimension_semantics=("arbitrary", "arbitrary"),
            vmem_limit_bytes=VMEM_LIMIT_BYTES),
        name="ada_mods",
    )(c, w_ada, b_ada.reshape(DEPTH, 1, n_cols))


def _zoh(a_re, a_im, log_step):
    lr = jnp.minimum(a_re, -1e-4)
    li = a_im
    dt = jnp.exp(log_step)
    mag = jnp.exp(lr * dt)
    abr = mag * jnp.cos(li * dt)
    abi = mag * jnp.sin(li * dt)
    den = lr * lr + li * li
    qr = ((abr - 1.0) * lr + abi * li) / den
    qi = (abi * lr - (abr - 1.0) * li) / den
    return abr, abi, qr, qi


def _ssm_param_kernel(al_ref, at_ref, bt_ref, ct_ref, wz_ref, wy_ref, a2_ref):
    bf16 = jnp.bfloat16
    abr, abi, qr, qi = _zoh(al_ref[0, 0], al_ref[1, 0], al_ref[2, 0])
    a2r = abr * abr - abi * abi
    a2i = 2.0 * abr * abi
    a2_ref[0, :, 0] = jnp.broadcast_to(a2r[:, None, :], (N_CHUNKS, SUBLANES, LANES))
    a2_ref[0, :, 1] = jnp.broadcast_to(a2i[:, None, :], (N_CHUNKS, SUBLANES, LANES))
    btr = bt_ref[0, 0]
    bti = bt_ref[1, 0]
    bbr = qr[:, None, :] * btr - qi[:, None, :] * bti
    bbi = qr[:, None, :] * bti + qi[:, None, :] * btr
    bar = abr[:, None, :] * bbr - abi[:, None, :] * bbi
    bai = abr[:, None, :] * bbi + abi[:, None, :] * bbr
    abr_t, abi_t, _, _ = _zoh(at_ref[0, 0], at_ref[1, 0], at_ref[2, 0])
    ctr = ct_ref[0, 0]
    cti = ct_ref[1, 0]

    row_group = lax.broadcasted_iota(jnp.int32, (LANES, LANES), 0) // SSM_GROUP
    col_group = lax.broadcasted_iota(jnp.int32, (LANES, LANES), 1) // SSM_GROUP
    row_sub = lax.broadcasted_iota(jnp.int32, (LANES, LANES), 0) // SSM_STATE
    col_sub = lax.broadcasted_iota(jnp.int32, (LANES, LANES), 1) // SSM_STATE
    lane_sub = lax.broadcasted_iota(jnp.int32, (SSM_GROUP, LANES), 1) // SSM_STATE
    spread = (lax.broadcasted_iota(jnp.int32, (SSM_GROUP, LANES), 1) % SSM_GROUP
              == lax.broadcasted_iota(jnp.int32, (SSM_GROUP, LANES), 0)).astype(bf16)

    def rows_by_group(block):
        rep = jnp.broadcast_to(block[None], (GROUPS_PER_COL, SSM_GROUP, block.shape[-1]))
        return rep.reshape(LANES, block.shape[-1])

    def cols_by_group(blocks):
        flat = blocks.reshape(-1, SSM_GROUP).astype(bf16)
        return jnp.dot(flat, spread, preferred_element_type=jnp.float32)

    for j in range(N_CHUNKS):
        c = j % CHUNKS_PER_COL
        b_live = row_group == c * GROUPS_PER_CHUNK + col_sub
        for half, (re, im) in enumerate(((bar, bai), (bbr, bbi))):
            rows = pl.ds(half * LANES, LANES)
            wz_ref[0, j, rows, pl.ds(0, LANES)] = (
                jnp.where(b_live, rows_by_group(re[j]), 0.0).astype(bf16))
            wz_ref[0, j, rows, pl.ds(LANES, LANES)] = (
                jnp.where(b_live, rows_by_group(im[j]), 0.0).astype(bf16))

    car, cai = [], []
    for j in range(N_CHUNKS):
        a_r, a_i = abr_t[:, j:j + 1], abi_t[:, j:j + 1]
        car.append(ctr[j] * a_r - cti[j] * a_i)
        cai.append(ctr[j] * a_i + cti[j] * a_r)
    car, cai = jnp.stack(car), jnp.stack(cai)
    spread_c = [(cols_by_group(ctr), cols_by_group(car)), (cols_by_group(-cti), cols_by_group(-cai))]
    for j in range(N_CHUNKS):
        v, c = divmod(j, CHUNKS_PER_COL)
        c_live = col_group == c * GROUPS_PER_CHUNK + row_sub
        for part, (even, odd) in enumerate(spread_c):
            rows = pl.ds(c * 2 * LANES + part * LANES, LANES)
            src_rows = slice(j * LANES, (j + 1) * LANES)
            wy_ref[0, v, rows, pl.ds(0, LANES)] = jnp.where(c_live, even[src_rows], 0.0).astype(bf16)
            wy_ref[0, v, rows, pl.ds(LANES, LANES)] = jnp.where(c_live, odd[src_rows], 0.0).astype(bf16)

    cb = []
    for l in range(GROUPS_PER_CHUNK):
        mine = (lane_sub == l)[None]
        cb.append(jnp.einsum('jhk,jko->jho', jnp.where(mine, bbr, 0.0), ctr,
                             preferred_element_type=jnp.float32)
                  - jnp.einsum('jhk,jko->jho', jnp.where(mine, bbi, 0.0), cti,
                               preferred_element_type=jnp.float32))
    cb = cols_by_group(jnp.stack(cb, axis=1))
    on_diag = row_group == col_group
    for v in range(N_COLS):
        rows = pl.ds(CHUNKS_PER_COL * 2 * LANES, LANES)
        wy_ref[0, v, rows, pl.ds(0, LANES)] = jnp.zeros((LANES, LANES), bf16)
        wy_ref[0, v, rows, pl.ds(LANES, LANES)] = (
            jnp.where(on_diag, cb[v * LANES:(v + 1) * LANES], 0.0).astype(bf16))


def _ssm_prep(a_re, a_im, log_step, b_re, b_im, c_re, c_im):
    n = a_re.shape[0]
    f32 = jnp.float32
    p, h = SSM_STATE, SSM_GROUP
    a_lanes = jnp.stack([a_re, a_im, jnp.broadcast_to(log_step[:, :, None], a_re.shape)])
    a_lanes = a_lanes.reshape(3, n, N_CHUNKS, LANES)
    a_rows = jnp.swapaxes(a_lanes, 2, 3)
    b_t = jnp.stack([b_re, b_im]).reshape(2, n, N_CHUNKS, GROUPS_PER_CHUNK, p, h)
    b_t = jnp.transpose(b_t, (0, 1, 2, 5, 3, 4)).reshape(2, n, N_CHUNKS, h, LANES)
    c_t = jnp.stack([c_re, c_im]).reshape(2, n, N_CHUNKS, GROUPS_PER_CHUNK, h, p)
    c_t = jnp.transpose(c_t, (0, 1, 2, 3, 5, 4)).reshape(2, n, N_CHUNKS, LANES, h)
    return pl.pallas_call(
        _ssm_param_kernel,
        grid=(n,),
        in_specs=[
            pl.BlockSpec((3, 1, N_CHUNKS, LANES), lambda i: (0, i, 0, 0)),
            pl.BlockSpec((3, 1, LANES, N_CHUNKS), lambda i: (0, i, 0, 0)),
            pl.BlockSpec((2, 1, N_CHUNKS, h, LANES), lambda i: (0, i, 0, 0, 0)),
            pl.BlockSpec((2, 1, N_CHUNKS, LANES, h), lambda i: (0, i, 0, 0, 0)),
        ],
        out_specs=(
            pl.BlockSpec((1, N_CHUNKS, 2 * LANES, 2 * LANES), lambda i: (i, 0, 0, 0)),
            pl.BlockSpec((1, N_COLS, COL_K, 2 * LANES), lambda i: (i, 0, 0, 0)),
            pl.BlockSpec((1, N_CHUNKS, 2, SUBLANES, LANES), lambda i: (i, 0, 0, 0, 0)),
        ),
        out_shape=(
            jax.ShapeDtypeStruct((n, N_CHUNKS, 2 * LANES, 2 * LANES), jnp.bfloat16),
            jax.ShapeDtypeStruct((n, N_COLS, COL_K, 2 * LANES), jnp.bfloat16),
            jax.ShapeDtypeStruct((n, N_CHUNKS, 2, SUBLANES, LANES), f32),
        ),
        compiler_params=pltpu.CompilerParams(
            dimension_semantics=("arbitrary",), vmem_limit_bytes=VMEM_LIMIT_BYTES),
        name="ssm_params",
    )(a_lanes, a_rows, b_t, c_t)


def _s5_kernel(x_ref, mods_ref, gain_ref, wz_ref, wy_ref, a2_ref, d_ref, wout_hbm,
               o_ref, z_ref, xs_ref, state_ref, oprev_ref, wout_ref, wout_stage, wout_sems,
               *fetch_scratch, pairs, n_blocks, w_layer):
    bf16 = jnp.bfloat16

    @pl.when(pl.program_id(0) == 0)
    def _():
        state_ref[...] = jnp.zeros_like(state_ref)
        oprev_ref[...] = jnp.zeros_like(oprev_ref)
        _stage_bf16(wout_hbm, w_layer, wout_ref, wout_stage, wout_sems)

    if fetch_scratch:
        x = _fetch_time_major(x_ref, *fetch_scratch, n_blocks)
    else:
        x = x_ref[...]
    mods = mods_ref[0]
    shift, scale, gate = (mods[:, k * D_MODEL:(k + 1) * D_MODEL] for k in range(3))

    half_rows = pairs * BATCH
    x_pairs = x.reshape(pairs, 2, BATCH, D_MODEL)
    x_even = x_pairs[:, 0].reshape(half_rows, D_MODEL)
    x_odd = x_pairs[:, 1].reshape(half_rows, D_MODEL)
    h_even = _norm_modulate(x_even, gain_ref[0], shift, scale)
    h_odd = _norm_modulate(x_odd, gain_ref[0], shift, scale)
    h_odd_prev = jnp.concatenate([oprev_ref[...], h_odd[:half_rows - BATCH]], axis=0)
    oprev_ref[...] = h_odd[half_rows - BATCH:]
    hb_even, hb_odd, hb_odd_prev = (t.astype(bf16) for t in (h_even, h_odd, h_odd_prev))

    ys_even, ys_odd = [], []
    for v in range(N_COLS):
        lanes = slice(v * LANES, (v + 1) * LANES)
        chunks = range(v * CHUNKS_PER_COL, (v + 1) * CHUNKS_PER_COL)
        lhs = jnp.concatenate([hb_odd_prev[:, lanes], hb_even[:, lanes]], axis=-1)
        for j in chunks:
            z_ref[j] = jnp.dot(lhs, wz_ref[0, j], preferred_element_type=jnp.float32)

        carry = [(state_ref[j, 0], state_ref[j, 1]) for j in chunks]
        for m0 in range(0, pairs, BF16_ROWS // SUBLANES):
            for c, j in enumerate(chunks):
                sr, si = carry[c]
                a_re, a_im = a2_ref[0, j, 0], a2_ref[0, j, 1]
                outs_r, outs_i = [], []
                for m in range(m0, m0 + BF16_ROWS // SUBLANES):
                    rows = pl.ds(m * SUBLANES, SUBLANES)
                    z_r = z_ref[j, rows, pl.ds(0, LANES)]
                    z_i = z_ref[j, rows, pl.ds(LANES, LANES)]
                    sr, si = (a_re * sr - a_im * si + z_r, a_re * si + a_im * sr + z_i)
                    outs_r.append(sr)
                    outs_i.append(si)
                carry[c] = (sr, si)
                rows = pl.ds(m0 * SUBLANES, BF16_ROWS)
                col = c * 2 * LANES
                xs_ref[v, rows, pl.ds(col, LANES)] = jnp.concatenate(outs_r, axis=0).astype(bf16)
                xs_ref[v, rows, pl.ds(col + LANES, LANES)] = (
                    jnp.concatenate(outs_i, axis=0).astype(bf16))
        for c, j in enumerate(chunks):
            state_ref[j, 0] = carry[c][0]
            state_ref[j, 1] = carry[c][1]

        lhs = jnp.concatenate([xs_ref[v], hb_odd[:, lanes]], axis=-1)
        y = jnp.dot(lhs, wy_ref[0, v], preferred_element_type=jnp.float32)
        ys_even.append(y[:, :LANES])
        ys_odd.append(y[:, LANES:])

    y = jnp.concatenate([jnp.concatenate(ys_even, axis=-1) + d_ref[0] * h_even,
                         jnp.concatenate(ys_odd, axis=-1) + d_ref[0] * h_odd], axis=0)
    y = jax.nn.gelu(y).astype(bf16)
    z = jnp.dot(y, wout_ref[...], preferred_element_type=jnp.float32)
    mix = z[:, :D_MODEL] * jax.nn.sigmoid(z[:, D_MODEL:])
    out = _gated_residual(jnp.concatenate([x_even, x_odd], axis=0), gate, mix)
    out_pairs = jnp.concatenate([out[:half_rows].reshape(pairs, 1, BATCH, D_MODEL),
                                 out[half_rows:].reshape(pairs, 1, BATCH, D_MODEL)], axis=1)
    o_ref[...] = out_pairs.reshape(2 * half_rows, D_MODEL)


def _s5_layer(x_in, mods, layer, j, gain, wz, wy, a2, d, w_out):
    steps = ROWS_S5 // BATCH
    pairs = steps // 2
    batch_major = x_in.ndim == 3
    rows = x_in.shape[0] * x_in.shape[1] if batch_major else x_in.shape[0]
    n_blocks = rows // ROWS_S5
    if batch_major:
        x_spec = pl.BlockSpec(memory_space=pl.ANY)
        fetch_scratch = [pltpu.VMEM((2, steps, BATCH, D_MODEL), jnp.float32),
                         pltpu.SemaphoreType.DMA((2, BATCH))]
    else:
        x_spec = pl.BlockSpec((ROWS_S5, D_MODEL), lambda i: (i, 0))
        fetch_scratch = []
    return pl.pallas_call(
        functools.partial(_s5_kernel, pairs=pairs, n_blocks=n_blocks, w_layer=j),
        grid=(n_blocks,),
        in_specs=[
            x_spec,
            pl.BlockSpec((1, BATCH, 6 * D_MODEL), lambda i: (layer, 0, 0)),
            pl.BlockSpec((1, 1, D_MODEL), lambda i: (layer, 0, 0)),
            _layer_spec(wz.shape, j),
            _layer_spec(wy.shape, j),
            _layer_spec(a2.shape, j),
            _layer_spec(d.shape, j),
            pl.BlockSpec(memory_space=pl.ANY),
        ],
        out_specs=pl.BlockSpec((ROWS_S5, D_MODEL), lambda i: (i, 0)),
        out_shape=jax.ShapeDtypeStruct((rows, D_MODEL), x_in.dtype),
        scratch_shapes=[
            pltpu.VMEM((N_CHUNKS, pairs * BATCH, 2 * LANES), jnp.float32),
            pltpu.VMEM((N_COLS, pairs * BATCH, CHUNKS_PER_COL * 2 * LANES), jnp.bfloat16),
            pltpu.VMEM((N_CHUNKS, 2, SUBLANES, LANES), jnp.float32),
            pltpu.VMEM((BATCH, D_MODEL), jnp.float32),
        ] + _stage_scratch(w_out.shape) + fetch_scratch,
        compiler_params=pltpu.CompilerParams(
            dimension_semantics=("arbitrary",), vmem_limit_bytes=VMEM_LIMIT_BYTES),
        name="s5_layer",
    )(x_in, mods, gain, wz, wy, a2, d, w_out)


def _conv_kernel(x_ref, mods_ref, gain_ref, win_hbm, cw_ref, wout_hbm, o_ref, cv_ref,
                 win_ref, win_stage, win_sems, wout_ref, wout_stage, wout_sems, *, w_layer):
    rows = x_ref.shape[0]
    hist = (CONV_WIDTH - 1) * BATCH

    @pl.when(pl.program_id(0) == 0)
    def _():
        cv_ref[pl.ds(0, hist), :] = jnp.zeros((hist, D_MODEL), jnp.float32)
        _stage_bf16(win_hbm, w_layer, win_ref, win_stage, win_sems)
        _stage_bf16(wout_hbm, w_layer, wout_ref, wout_stage, wout_sems)

    x = x_ref[...]
    mods = mods_ref[0]
    shift, scale, gate = (mods[:, k * D_MODEL:(k + 1) * D_MODEL] for k in range(3))
    h = _norm_modulate(x, gain_ref[0], shift, scale)
    proj = jnp.dot(h.astype(jnp.bfloat16), win_ref[...], preferred_element_type=jnp.float32)
    bg = proj[:, :D_MODEL]
    cv = proj[:, D_MODEL:2 * D_MODEL] * proj[:, 2 * D_MODEL:]
    cv_ref[pl.ds(hist, rows), :] = cv
    cw = cw_ref[0]
    conv = (cw[2:3] * cv + cw[1:2] * cv_ref[pl.ds(BATCH, rows), :]
            + cw[0:1] * cv_ref[pl.ds(0, rows), :])
    cv_ref[pl.ds(0, hist), :] = cv_ref[pl.ds(rows, hist), :]
    out = jnp.dot((bg * conv).astype(jnp.bfloat16), wout_ref[...],
                  preferred_element_type=jnp.float32)
    o_ref[...] = _gated_residual(x, gate, out)


def _conv_layer(x2, mods, layer, j, gain, w_in, conv_w, w_out):
    rows = x2.shape[0]
    hist = (CONV_WIDTH - 1) * BATCH
    return pl.pallas_call(
        functools.partial(_conv_kernel, w_layer=j),
        grid=(rows // ROWS_CONV,),
        in_specs=[
            pl.BlockSpec((ROWS_CONV, D_MODEL), lambda i: (i, 0)),
            pl.BlockSpec((1, BATCH, 6 * D_MODEL), lambda i: (layer, 0, 0)),
            pl.BlockSpec((1, 1, D_MODEL), lambda i: (layer, 0, 0)),
            pl.BlockSpec(memory_space=pl.ANY),
            _layer_spec(conv_w.shape, j),
            pl.BlockSpec(memory_space=pl.ANY),
        ],
        out_specs=pl.BlockSpec((ROWS_CONV, D_MODEL), lambda i: (i, 0)),
        out_shape=jax.ShapeDtypeStruct(x2.shape, x2.dtype),
        scratch_shapes=([pltpu.VMEM((ROWS_CONV + hist, D_MODEL), jnp.float32)]
                        + _stage_scratch(w_in.shape) + _stage_scratch(w_out.shape)),
        compiler_params=pltpu.CompilerParams(
            dimension_semantics=("arbitrary",), vmem_limit_bytes=VMEM_LIMIT_BYTES),
        name="conv_layer",
    )(x2, mods, gain, w_in, conv_w, w_out)


def _ffn_kernel(x_ref, mods_ref, gain_ref, win_hbm, wout_hbm, fg_ref, o_ref,
                win_ref, win_stage, win_sems, wout_ref, wout_stage, wout_sems, *store_scratch,
                final_norm, n_blocks, w_layer):
    hidden = wout_ref.shape[0]

    @pl.when(pl.program_id(0) == 0)
    def _():
        _stage_bf16(win_hbm, w_layer, win_ref, win_stage, win_sems)
        _stage_bf16(wout_hbm, w_layer, wout_ref, wout_stage, wout_sems)

    mods = mods_ref[0]
    shift, scale, gate = (mods[:, k * D_MODEL:(k + 1) * D_MODEL] for k in range(3, 6))
    sub_rows = x_ref.shape[0] // FFN_SUB_BLOCKS
    halves = []
    for q in range(FFN_SUB_BLOCKS):
        rows = pl.ds(q * sub_rows, sub_rows)
        x = x_ref[rows, :]
        h = _norm_modulate(x, gain_ref[0], shift, scale)
        gu = jnp.dot(h.astype(jnp.bfloat16), win_ref[...], preferred_element_type=jnp.float32)
        act = (jax.nn.silu(gu[:, :hidden]) * gu[:, hidden:]).astype(jnp.bfloat16)
        out = jnp.dot(act, wout_ref[...], preferred_element_type=jnp.float32)
        xn = _gated_residual(x, gate, out)
        if final_norm:
            xn = xn * lax.rsqrt(jnp.mean(xn * xn, axis=-1, keepdims=True) + RMS_EPS) * fg_ref[...]
        if store_scratch:
            halves.append(xn)
        else:
            o_ref[rows, :] = xn
    if store_scratch:
        _store_batch_major(jnp.concatenate(halves, axis=0), o_ref, *store_scratch, n_blocks)


def _ffn_layer(x2, mods, layer, gain, w_in, w_out, final_g, final_norm, batch_major_out):
    rows = x2.shape[0]
    n_blocks = rows // ROWS_FFN
    if batch_major_out:
        steps = ROWS_FFN // BATCH
        out_spec = pl.BlockSpec(memory_space=pl.ANY)
        out_shape = jax.ShapeDtypeStruct((BATCH, rows // BATCH, D_MODEL), x2.dtype)
        store_scratch = [pltpu.VMEM((2, steps, BATCH, D_MODEL), jnp.float32),
                         pltpu.SemaphoreType.DMA((2, BATCH))]
    else:
        out_spec = pl.BlockSpec((ROWS_FFN, D_MODEL), lambda i: (i, 0))
        out_shape = jax.ShapeDtypeStruct(x2.shape, x2.dtype)
        store_scratch = []
    return pl.pallas_call(
        functools.partial(_ffn_kernel, final_norm=final_norm, n_blocks=n_blocks, w_layer=layer),
        grid=(n_blocks,),
        in_specs=[
            pl.BlockSpec((ROWS_FFN, D_MODEL), lambda i: (i, 0)),
            pl.BlockSpec((1, BATCH, 6 * D_MODEL), lambda i: (layer, 0, 0)),
            pl.BlockSpec((1, 1, D_MODEL), lambda i: (layer, 0, 0)),
            pl.BlockSpec(memory_space=pl.ANY),
            pl.BlockSpec(memory_space=pl.ANY),
            _const_spec((1, D_MODEL)),
        ],
        out_specs=out_spec,
        out_shape=out_shape,
        scratch_shapes=_stage_scratch(w_in.shape) + _stage_scratch(w_out.shape) + store_scratch,
        compiler_params=pltpu.CompilerParams(
            dimension_semantics=("arbitrary",), vmem_limit_bytes=VMEM_LIMIT_BYTES),
        name="ffn_layer",
    )(x2, mods, gain, w_in, w_out, final_g.reshape(1, D_MODEL))


def kernel(x, c, norm1_g, norm2_g, w_ada, b_ada, ssm_a_re, ssm_a_im, ssm_log_step, ssm_b_re, ssm_b_im, ssm_c_re, ssm_c_im, ssm_d, ssm_w_out, conv_w_in, conv_w, conv_w_out, w_ffn_in, w_ffn_out, final_g):
    mods = _ada_mods(c, w_ada, b_ada)
    gain1 = norm1_g.reshape(DEPTH, 1, D_MODEL)
    gain2 = norm2_g.reshape(DEPTH, 1, D_MODEL)
    wz, wy, a2 = _ssm_prep(ssm_a_re, ssm_a_im, ssm_log_step, ssm_b_re, ssm_b_im,
                          ssm_c_re, ssm_c_im)
    ssm_d3 = ssm_d.reshape(-1, 1, D_MODEL)
    x2 = x
    for i in range(DEPTH):
        j = i // N_MIXERS
        if i % N_MIXERS == 0:
            x2 = _s5_layer(x2, mods, i, j, gain1, wz, wy, a2, ssm_d3, ssm_w_out)
        else:
            x2 = _conv_layer(x2, mods, i, j, gain1, conv_w_in, conv_w, conv_w_out)
        last = i == DEPTH - 1
        x2 = _ffn_layer(x2, mods, i, gain2, w_ffn_in, w_ffn_out, final_g,
                        final_norm=last, batch_major_out=last)
    return x2
```

```python
import functools

import jax
import jax.numpy as jnp
from jax import lax
from jax.experimental import pallas as pl
from jax.experimental.pallas import tpu as pltpu

D_MODEL = 1024
BATCH = 8
DEPTH = 4
N_MIXERS = 2
SSM_GROUP = 16
SSM_GROUPS = D_MODEL // SSM_GROUP
SSM_STATE = 64
CONV_WIDTH = 3
RMS_EPS = 1e-6

LANES = 128
SUBLANES = 8
BF16_ROWS = 16
VMEM_LIMIT_BYTES = 56 * 1024 * 1024
STAGE_BYTES = 3 * 512 * 1024

N_CHUNKS = SSM_GROUPS * SSM_STATE // LANES
GROUPS_PER_CHUNK = LANES // SSM_STATE
N_COLS = D_MODEL // LANES
GROUPS_PER_COL = LANES // SSM_GROUP
CHUNKS_PER_COL = N_CHUNKS // N_COLS
COL_K = CHUNKS_PER_COL * 2 * LANES + LANES

ROWS_FFN = 1024
FFN_SUB_BLOCKS = 4
ROWS_CONV = 1024
ROWS_S5 = 512
FETCH_SLOTS = 3
ADA_COLS = 3072


def _const_spec(shape, index=None):
    index = (0,) * len(shape) if index is None else index
    return pl.BlockSpec(shape, lambda *_: index, pipeline_mode=pl.Buffered(1))


def _layer_spec(shape, layer):
    return _const_spec((1,) + tuple(shape[1:]), (layer,) + (0,) * (len(shape) - 1))


def _norm_modulate(x, gain, shift, scale):
    rows = x.shape[0]
    y = x * lax.rsqrt(jnp.mean(x * x, axis=-1, keepdims=True) + RMS_EPS)
    y = y.reshape(rows // BATCH, BATCH, D_MODEL)
    return (y * (gain * (1.0 + scale))[None] + shift[None]).reshape(rows, D_MODEL)


def _gated_residual(x, gate, update):
    rows = x.shape[0]
    upd = update.reshape(rows // BATCH, BATCH, D_MODEL) * gate[None]
    return x + upd.reshape(rows, D_MODEL)


def _stage_scratch(shape):
    k_dim, n_dim = shape[-2:]
    fits = [r for r in range(BF16_ROWS, k_dim + 1, BF16_ROWS)
            if k_dim % r == 0 and r * n_dim * 4 <= STAGE_BYTES]
    return [pltpu.VMEM((k_dim, n_dim), jnp.bfloat16),
            pltpu.VMEM((2, max(fits), n_dim), jnp.float32),
            pltpu.SemaphoreType.DMA((2,))]


def _stage_bf16(w_hbm, layer, w_bf16, stage, sems):
    chunk = stage.shape[1]
    n_chunks = w_bf16.shape[0] // chunk

    def copy(k):
        return pltpu.make_async_copy(w_hbm.at[layer, pl.ds(k * chunk, chunk), :],
                                     stage.at[k % 2], sems.at[k % 2])

    copy(0).start()
    for k in range(n_chunks):
        if k + 1 < n_chunks:
            copy(k + 1).start()
        copy(k).wait()
        w_bf16[pl.ds(k * chunk, chunk), :] = stage[k % 2].astype(jnp.bfloat16)


def _batch_row_copies(hbm, buf, sems, block, slot, to_hbm):
    steps = buf.shape[1]
    copies = []
    for b in range(BATCH):
        hbm_rows = hbm.at[b, pl.ds(block * steps, steps), :]
        vmem_rows = buf.at[slot, :, b, :]
        src, dst = (vmem_rows, hbm_rows) if to_hbm else (hbm_rows, vmem_rows)
        copies.append(pltpu.make_async_copy(src, dst, sems.at[slot, b]))
    return copies


def _fetch_time_major(x_hbm, buf, sems, n_blocks):
    i = pl.program_id(0)
    slots = buf.shape[0]
    ahead = slots - 1

    @pl.when(i == 0)
    def _():
        for k in range(min(ahead, n_blocks)):
            for c in _batch_row_copies(x_hbm, buf, sems, k, k, to_hbm=False):
                c.start()

    @pl.when(i + ahead < n_blocks)
    def _():
        for c in _batch_row_copies(x_hbm, buf, sems, i + ahead, (i + ahead) % slots, to_hbm=False):
            c.start()

    for c in _batch_row_copies(x_hbm, buf, sems, i, i % slots, to_hbm=False):
        c.wait()
    return buf[i % slots].reshape(buf.shape[1] * BATCH, D_MODEL)


def _store_batch_major(value, o_hbm, buf, sems, n_blocks):
    i = pl.program_id(0)
    slot = i % 2

    @pl.when(i >= 2)
    def _():
        for c in _batch_row_copies(o_hbm, buf, sems, i - 2, slot, to_hbm=True):
            c.wait()

    buf[slot] = value.reshape(buf.shape[1], BATCH, D_MODEL)
    for c in _batch_row_copies(o_hbm, buf, sems, i, slot, to_hbm=True):
        c.start()

    @pl.when(i == n_blocks - 1)
    def _():
        for back in range(min(2, n_blocks)):
            for c in _batch_row_copies(o_hbm, buf, sems, i - back, (i - back) % 2, to_hbm=True):
                c.wait()


def _ada_kernel(c_ref, w_ref, b_ref, o_ref):
    c_act = jax.nn.silu(c_ref[...])
    o_ref[0] = jnp.dot(c_act, w_ref[0], preferred_element_type=jnp.float32) + b_ref[0]


def _ada_mods(c, w_ada, b_ada):
    n_cols = 6 * D_MODEL
    return pl.pallas_call(
        _ada_kernel,
        grid=(DEPTH, n_cols // ADA_COLS),
        in_specs=[
            pl.BlockSpec((BATCH, D_MODEL), lambda i, j: (0, 0)),
            pl.BlockSpec((1, D_MODEL, ADA_COLS), lambda i, j: (i, 0, j)),
            pl.BlockSpec((1, 1, ADA_COLS), lambda i, j: (i, 0, j)),
        ],
        out_specs=pl.BlockSpec((1, BATCH, ADA_COLS), lambda i, j: (i, 0, j)),
        out_shape=jax.ShapeDtypeStruct((DEPTH, BATCH, n_cols), jnp.float32),
        compiler_params=pltpu.CompilerParams(
            dimension_semantics=("arbitrary", "arbitrary"),
            vmem_limit_bytes=VMEM_LIMIT_BYTES),
        name="ada_mods",
    )(c, w_ada, b_ada.reshape(DEPTH, 1, n_cols))


def _zoh(a_re, a_im, log_step):
    lr = jnp.minimum(a_re, -1e-4)
    li = a_im
    dt = jnp.exp(log_step)
    mag = jnp.exp(lr * dt)
    abr = mag * jnp.cos(li * dt)
    abi = mag * jnp.sin(li * dt)
    den = lr * lr + li * li
    qr = ((abr - 1.0) * lr + abi * li) / den
    qi = (abi * lr - (abr - 1.0) * li) / den
    return abr, abi, qr, qi


def _ssm_param_kernel(al_ref, at_ref, bt_ref, ct_ref, wz_ref, wy_ref, a2_ref):
    bf16 = jnp.bfloat16
    abr, abi, qr, qi = _zoh(al_ref[0, 0], al_ref[1, 0], al_ref[2, 0])
    a2r = abr * abr - abi * abi
    a2i = 2.0 * abr * abi
    a2_ref[0, :, 0] = jnp.broadcast_to(a2r[:, None, :], (N_CHUNKS, SUBLANES, LANES))
    a2_ref[0, :, 1] = jnp.broadcast_to(a2i[:, None, :], (N_CHUNKS, SUBLANES, LANES))
    btr = bt_ref[0, 0]
    bti = bt_ref[1, 0]
    bbr = qr[:, None, :] * btr - qi[:, None, :] * bti
    bbi = qr[:, None, :] * bti + qi[:, None, :] * btr
    bar = abr[:, None, :] * bbr - abi[:, None, :] * bbi
    bai = abr[:, None, :] * bbi + abi[:, None, :] * bbr
    abr_t, abi_t, _, _ = _zoh(at_ref[0, 0], at_ref[1, 0], at_ref[2, 0])
    ctr = ct_ref[0, 0]
    cti = ct_ref[1, 0]

    row_group = lax.broadcasted_iota(jnp.int32, (LANES, LANES), 0) // SSM_GROUP
    col_group = lax.broadcasted_iota(jnp.int32, (LANES, LANES), 1) // SSM_GROUP
    row_sub = lax.broadcasted_iota(jnp.int32, (LANES, LANES), 0) // SSM_STATE
    col_sub = lax.broadcasted_iota(jnp.int32, (LANES, LANES), 1) // SSM_STATE
    lane_sub = lax.broadcasted_iota(jnp.int32, (SSM_GROUP, LANES), 1) // SSM_STATE
    spread = (lax.broadcasted_iota(jnp.int32, (SSM_GROUP, LANES), 1) % SSM_GROUP
              == lax.broadcasted_iota(jnp.int32, (SSM_GROUP, LANES), 0)).astype(bf16)

    def rows_by_group(block):
        rep = jnp.broadcast_to(block[None], (GROUPS_PER_COL, SSM_GROUP, block.shape[-1]))
        return rep.reshape(LANES, block.shape[-1])

    def cols_by_group(blocks):
        flat = blocks.reshape(-1, SSM_GROUP).astype(bf16)
        return jnp.dot(flat, spread, preferred_element_type=jnp.float32)

    for j in range(N_CHUNKS):
        c = j % CHUNKS_PER_COL
        b_live = row_group == c * GROUPS_PER_CHUNK + col_sub
        for half, (re, im) in enumerate(((bar, bai), (bbr, bbi))):
            rows = pl.ds(half * LANES, LANES)
            wz_ref[0, j, rows, pl.ds(0, LANES)] = (
                jnp.where(b_live, rows_by_group(re[j]), 0.0).astype(bf16))
            wz_ref[0, j, rows, pl.ds(LANES, LANES)] = (
                jnp.where(b_live, rows_by_group(im[j]), 0.0).astype(bf16))

    car, cai = [], []
    for j in range(N_CHUNKS):
        a_r, a_i = abr_t[:, j:j + 1], abi_t[:, j:j + 1]
        car.append(ctr[j] * a_r - cti[j] * a_i)
        cai.append(ctr[j] * a_i + cti[j] * a_r)
    car, cai = jnp.stack(car), jnp.stack(cai)
    spread_c = [(cols_by_group(ctr), cols_by_group(car)), (cols_by_group(-cti), cols_by_group(-cai))]
    for j in range(N_CHUNKS):
        v, c = divmod(j, CHUNKS_PER_COL)
        c_live = col_group == c * GROUPS_PER_CHUNK + row_sub
        for part, (even, odd) in enumerate(spread_c):
            rows = pl.ds(c * 2 * LANES + part * LANES, LANES)
            src_rows = slice(j * LANES, (j + 1) * LANES)
            wy_ref[0, v, rows, pl.ds(0, LANES)] = jnp.where(c_live, even[src_rows], 0.0).astype(bf16)
            wy_ref[0, v, rows, pl.ds(LANES, LANES)] = jnp.where(c_live, odd[src_rows], 0.0).astype(bf16)

    cb = []
    for l in range(GROUPS_PER_CHUNK):
        mine = (lane_sub == l)[None]
        cb.append(jnp.einsum('jhk,jko->jho', jnp.where(mine, bbr, 0.0), ctr,
                             preferred_element_type=jnp.float32)
                  - jnp.einsum('jhk,jko->jho', jnp.where(mine, bbi, 0.0), cti,
                               preferred_element_type=jnp.float32))
    cb = cols_by_group(jnp.stack(cb, axis=1))
    on_diag = row_group == col_group
    for v in range(N_COLS):
        rows = pl.ds(CHUNKS_PER_COL * 2 * LANES, LANES)
        wy_ref[0, v, rows, pl.ds(0, LANES)] = jnp.zeros((LANES, LANES), bf16)
        wy_ref[0, v, rows, pl.ds(LANES, LANES)] = (
            jnp.where(on_diag, cb[v * LANES:(v + 1) * LANES], 0.0).astype(bf16))


def _ssm_prep(a_re, a_im, log_step, b_re, b_im, c_re, c_im):
    n = a_re.shape[0]
    f32 = jnp.float32
    p, h = SSM_STATE, SSM_GROUP
    a_lanes = jnp.stack([a_re, a_im, jnp.broadcast_to(log_step[:, :, None], a_re.shape)])
    a_lanes = a_lanes.reshape(3, n, N_CHUNKS, LANES)
    a_rows = jnp.swapaxes(a_lanes, 2, 3)
    b_t = jnp.stack([b_re, b_im]).reshape(2, n, N_CHUNKS, GROUPS_PER_CHUNK, p, h)
    b_t = jnp.transpose(b_t, (0, 1, 2, 5, 3, 4)).reshape(2, n, N_CHUNKS, h, LANES)
    c_t = jnp.stack([c_re, c_im]).reshape(2, n, N_CHUNKS, GROUPS_PER_CHUNK, h, p)
    c_t = jnp.transpose(c_t, (0, 1, 2, 3, 5, 4)).reshape(2, n, N_CHUNKS, LANES, h)
    return pl.pallas_call(
        _ssm_param_kernel,
        grid=(n,),
        in_specs=[
            pl.BlockSpec((3, 1, N_CHUNKS, LANES), lambda i: (0, i, 0, 0)),
            pl.BlockSpec((3, 1, LANES, N_CHUNKS), lambda i: (0, i, 0, 0)),
            pl.BlockSpec((2, 1, N_CHUNKS, h, LANES), lambda i: (0, i, 0, 0, 0)),
            pl.BlockSpec((2, 1, N_CHUNKS, LANES, h), lambda i: (0, i, 0, 0, 0)),
        ],
        out_specs=(
            pl.BlockSpec((1, N_CHUNKS, 2 * LANES, 2 * LANES), lambda i: (i, 0, 0, 0)),
            pl.BlockSpec((1, N_COLS, COL_K, 2 * LANES), lambda i: (i, 0, 0, 0)),
            pl.BlockSpec((1, N_CHUNKS, 2, SUBLANES, LANES), lambda i: (i, 0, 0, 0, 0)),
        ),
        out_shape=(
            jax.ShapeDtypeStruct((n, N_CHUNKS, 2 * LANES, 2 * LANES), jnp.bfloat16),
            jax.ShapeDtypeStruct((n, N_COLS, COL_K, 2 * LANES), jnp.bfloat16),
            jax.ShapeDtypeStruct((n, N_CHUNKS, 2, SUBLANES, LANES), f32),
        ),
        compiler_params=pltpu.CompilerParams(
            dimension_semantics=("arbitrary",), vmem_limit_bytes=VMEM_LIMIT_BYTES),
        name="ssm_params",
    )(a_lanes, a_rows, b_t, c_t)


def _s5_kernel(x_ref, mods_ref, gain_ref, wz_ref, wy_ref, a2_ref, d_ref, wout_hbm,
               o_ref, z_ref, xs_ref, state_ref, oprev_ref, wout_ref, wout_stage, wout_sems,
               *fetch_scratch, pairs, n_blocks, w_layer):
    bf16 = jnp.bfloat16

    @pl.when(pl.program_id(0) == 0)
    def _():
        state_ref[...] = jnp.zeros_like(state_ref)
        oprev_ref[...] = jnp.zeros_like(oprev_ref)
        _stage_bf16(wout_hbm, w_layer, wout_ref, wout_stage, wout_sems)

    if fetch_scratch:
        x = _fetch_time_major(x_ref, *fetch_scratch, n_blocks)
    else:
        x = x_ref[...]
    mods = mods_ref[0]
    shift, scale, gate = (mods[:, k * D_MODEL:(k + 1) * D_MODEL] for k in range(3))

    half_rows = pairs * BATCH
    x_pairs = x.reshape(pairs, 2, BATCH, D_MODEL)
    x_even = x_pairs[:, 0].reshape(half_rows, D_MODEL)
    x_odd = x_pairs[:, 1].reshape(half_rows, D_MODEL)
    h_even = _norm_modulate(x_even, gain_ref[0], shift, scale)
    h_odd = _norm_modulate(x_odd, gain_ref[0], shift, scale)
    h_odd_prev = jnp.concatenate([oprev_ref[...], h_odd[:half_rows - BATCH]], axis=0)
    oprev_ref[...] = h_odd[half_rows - BATCH:]
    hb_even, hb_odd, hb_odd_prev = (t.astype(bf16) for t in (h_even, h_odd, h_odd_prev))

    ys_even, ys_odd = [], []
    for v in range(N_COLS):
        lanes = slice(v * LANES, (v + 1) * LANES)
        chunks = range(v * CHUNKS_PER_COL, (v + 1) * CHUNKS_PER_COL)
        lhs = jnp.concatenate([hb_odd_prev[:, lanes], hb_even[:, lanes]], axis=-1)
        for j in chunks:
            z_ref[j] = jnp.dot(lhs, wz_ref[0, j], preferred_element_type=jnp.float32)

        carry = [(state_ref[j, 0], state_ref[j, 1]) for j in chunks]
        for m0 in range(0, pairs, BF16_ROWS // SUBLANES):
            for c, j in enumerate(chunks):
                sr, si = carry[c]
                a_re, a_im = a2_ref[0, j, 0], a2_ref[0, j, 1]
                outs_r, outs_i = [], []
                for m in range(m0, m0 + BF16_ROWS // SUBLANES):
                    rows = pl.ds(m * SUBLANES, SUBLANES)
                    z_r = z_ref[j, rows, pl.ds(0, LANES)]
                    z_i = z_ref[j, rows, pl.ds(LANES, LANES)]
                    sr, si = (a_re * sr - a_im * si + z_r, a_re * si + a_im * sr + z_i)
                    outs_r.append(sr)
                    outs_i.append(si)
                carry[c] = (sr, si)
                rows = pl.ds(m0 * SUBLANES, BF16_ROWS)
                col = c * 2 * LANES
                xs_ref[v, rows, pl.ds(col, LANES)] = jnp.concatenate(outs_r, axis=0).astype(bf16)
                xs_ref[v, rows, pl.ds(col + LANES, LANES)] = (
                    jnp.concatenate(outs_i, axis=0).astype(bf16))
        for c, j in enumerate(chunks):
            state_ref[j, 0] = carry[c][0]
            state_ref[j, 1] = carry[c][1]

        lhs = jnp.concatenate([xs_ref[v], hb_odd[:, lanes]], axis=-1)
        y = jnp.dot(lhs, wy_ref[0, v], preferred_element_type=jnp.float32)
        ys_even.append(y[:, :LANES])
        ys_odd.append(y[:, LANES:])

    y = jnp.concatenate([jnp.concatenate(ys_even, axis=-1) + d_ref[0] * h_even,
                         jnp.concatenate(ys_odd, axis=-1) + d_ref[0] * h_odd], axis=0)
    y = jax.nn.gelu(y).astype(bf16)
    z = jnp.dot(y, wout_ref[...], preferred_element_type=jnp.float32)
    mix = z[:, :D_MODEL] * jax.nn.sigmoid(z[:, D_MODEL:])
    out = _gated_residual(jnp.concatenate([x_even, x_odd], axis=0), gate, mix)
    out_pairs = jnp.concatenate([out[:half_rows].reshape(pairs, 1, BATCH, D_MODEL),
                                 out[half_rows:].reshape(pairs, 1, BATCH, D_MODEL)], axis=1)
    o_ref[...] = out_pairs.reshape(2 * half_rows, D_MODEL)


def _s5_layer(x_in, mods, layer, j, gain, wz, wy, a2, d, w_out):
    steps = ROWS_S5 // BATCH
    pairs = steps // 2
    batch_major = x_in.ndim == 3
    rows = x_in.shape[0] * x_in.shape[1] if batch_major else x_in.shape[0]
    n_blocks = rows // ROWS_S5
    if batch_major:
        x_spec = pl.BlockSpec(memory_space=pl.ANY)
        fetch_scratch = [pltpu.VMEM((FETCH_SLOTS, steps, BATCH, D_MODEL), jnp.float32),
                         pltpu.SemaphoreType.DMA((FETCH_SLOTS, BATCH))]
    else:
        x_spec = pl.BlockSpec((ROWS_S5, D_MODEL), lambda i: (i, 0))
        fetch_scratch = []
    return pl.pallas_call(
        functools.partial(_s5_kernel, pairs=pairs, n_blocks=n_blocks, w_layer=j),
        grid=(n_blocks,),
        in_specs=[
            x_spec,
            pl.BlockSpec((1, BATCH, 6 * D_MODEL), lambda i: (layer, 0, 0)),
            pl.BlockSpec((1, 1, D_MODEL), lambda i: (layer, 0, 0)),
            _layer_spec(wz.shape, j),
            _layer_spec(wy.shape, j),
            _layer_spec(a2.shape, j),
            _layer_spec(d.shape, j),
            pl.BlockSpec(memory_space=pl.ANY),
        ],
        out_specs=pl.BlockSpec((ROWS_S5, D_MODEL), lambda i: (i, 0)),
        out_shape=jax.ShapeDtypeStruct((rows, D_MODEL), x_in.dtype),
        scratch_shapes=[
            pltpu.VMEM((N_CHUNKS, pairs * BATCH, 2 * LANES), jnp.float32),
            pltpu.VMEM((N_COLS, pairs * BATCH, CHUNKS_PER_COL * 2 * LANES), jnp.bfloat16),
            pltpu.VMEM((N_CHUNKS, 2, SUBLANES, LANES), jnp.float32),
            pltpu.VMEM((BATCH, D_MODEL), jnp.float32),
        ] + _stage_scratch(w_out.shape) + fetch_scratch,
        compiler_params=pltpu.CompilerParams(
            dimension_semantics=("arbitrary",), vmem_limit_bytes=VMEM_LIMIT_BYTES),
        name="s5_layer",
    )(x_in, mods, gain, wz, wy, a2, d, w_out)


def _conv_kernel(x_ref, mods_ref, gain_ref, win_hbm, cw_ref, wout_hbm, o_ref, cv_ref,
                 win_ref, win_stage, win_sems, wout_ref, wout_stage, wout_sems, *, w_layer):
    rows = x_ref.shape[0]
    hist = (CONV_WIDTH - 1) * BATCH

    @pl.when(pl.program_id(0) == 0)
    def _():
        cv_ref[pl.ds(0, hist), :] = jnp.zeros((hist, D_MODEL), jnp.float32)
        _stage_bf16(win_hbm, w_layer, win_ref, win_stage, win_sems)
        _stage_bf16(wout_hbm, w_layer, wout_ref, wout_stage, wout_sems)

    x = x_ref[...]
    mods = mods_ref[0]
    shift, scale, gate = (mods[:, k * D_MODEL:(k + 1) * D_MODEL] for k in range(3))
    h = _norm_modulate(x, gain_ref[0], shift, scale)
    proj = jnp.dot(h.astype(jnp.bfloat16), win_ref[...], preferred_element_type=jnp.float32)
    bg = proj[:, :D_MODEL]
    cv = proj[:, D_MODEL:2 * D_MODEL] * proj[:, 2 * D_MODEL:]
    cv_ref[pl.ds(hist, rows), :] = cv
    cw = cw_ref[0]
    conv = (cw[2:3] * cv + cw[1:2] * cv_ref[pl.ds(BATCH, rows), :]
            + cw[0:1] * cv_ref[pl.ds(0, rows), :])
    cv_ref[pl.ds(0, hist), :] = cv_ref[pl.ds(rows, hist), :]
    out = jnp.dot((bg * conv).astype(jnp.bfloat16), wout_ref[...],
                  preferred_element_type=jnp.float32)
    o_ref[...] = _gated_residual(x, gate, out)


def _conv_layer(x2, mods, layer, j, gain, w_in, conv_w, w_out):
    rows = x2.shape[0]
    hist = (CONV_WIDTH - 1) * BATCH
    return pl.pallas_call(
        functools.partial(_conv_kernel, w_layer=j),
        grid=(rows // ROWS_CONV,),
        in_specs=[
            pl.BlockSpec((ROWS_CONV, D_MODEL), lambda i: (i, 0)),
            pl.BlockSpec((1, BATCH, 6 * D_MODEL), lambda i: (layer, 0, 0)),
            pl.BlockSpec((1, 1, D_MODEL), lambda i: (layer, 0, 0)),
            pl.BlockSpec(memory_space=pl.ANY),
            _layer_spec(conv_w.shape, j),
            pl.BlockSpec(memory_space=pl.ANY),
        ],
        out_specs=pl.BlockSpec((ROWS_CONV, D_MODEL), lambda i: (i, 0)),
        out_shape=jax.ShapeDtypeStruct(x2.shape, x2.dtype),
        scratch_shapes=([pltpu.VMEM((ROWS_CONV + hist, D_MODEL), jnp.float32)]
                        + _stage_scratch(w_in.shape) + _stage_scratch(w_out.shape)),
        compiler_params=pltpu.CompilerParams(
            dimension_semantics=("arbitrary",), vmem_limit_bytes=VMEM_LIMIT_BYTES),
        name="conv_layer",
    )(x2, mods, gain, w_in, conv_w, w_out)


def _ffn_kernel(x_ref, mods_ref, gain_ref, win_hbm, wout_hbm, fg_ref, o_ref,
                win_ref, win_stage, win_sems, wout_ref, wout_stage, wout_sems, *store_scratch,
                final_norm, n_blocks, w_layer):
    hidden = wout_ref.shape[0]

    @pl.when(pl.program_id(0) == 0)
    def _():
        _stage_bf16(win_hbm, w_layer, win_ref, win_stage, win_sems)
        _stage_bf16(wout_hbm, w_layer, wout_ref, wout_stage, wout_sems)

    mods = mods_ref[0]
    shift, scale, gate = (mods[:, k * D_MODEL:(k + 1) * D_MODEL] for k in range(3, 6))
    sub_rows = x_ref.shape[0] // FFN_SUB_BLOCKS
    halves = []
    for q in range(FFN_SUB_BLOCKS):
        rows = pl.ds(q * sub_rows, sub_rows)
        x = x_ref[rows, :]
        h = _norm_modulate(x, gain_ref[0], shift, scale)
        gu = jnp.dot(h.astype(jnp.bfloat16), win_ref[...], preferred_element_type=jnp.float32)
        act = (jax.nn.silu(gu[:, :hidden]) * gu[:, hidden:]).astype(jnp.bfloat16)
        out = jnp.dot(act, wout_ref[...], preferred_element_type=jnp.float32)
        xn = _gated_residual(x, gate, out)
        if final_norm:
            xn = xn * lax.rsqrt(jnp.mean(xn * xn, axis=-1, keepdims=True) + RMS_EPS) * fg_ref[...]
        if store_scratch:
            halves.append(xn)
        else:
            o_ref[rows, :] = xn
    if store_scratch:
        _store_batch_major(jnp.concatenate(halves, axis=0), o_ref, *store_scratch, n_blocks)


def _ffn_layer(x2, mods, layer, gain, w_in, w_out, final_g, final_norm, batch_major_out):
    rows = x2.shape[0]
    n_blocks = rows // ROWS_FFN
    if batch_major_out:
        steps = ROWS_FFN // BATCH
        out_spec = pl.BlockSpec(memory_space=pl.ANY)
        out_shape = jax.ShapeDtypeStruct((BATCH, rows // BATCH, D_MODEL), x2.dtype)
        store_scratch = [pltpu.VMEM((2, steps, BATCH, D_MODEL), jnp.float32),
                         pltpu.SemaphoreType.DMA((2, BATCH))]
    else:
        out_spec = pl.BlockSpec((ROWS_FFN, D_MODEL), lambda i: (i, 0))
        out_shape = jax.ShapeDtypeStruct(x2.shape, x2.dtype)
        store_scratch = []
    return pl.pallas_call(
        functools.partial(_ffn_kernel, final_norm=final_norm, n_blocks=n_blocks, w_layer=layer),
        grid=(n_blocks,),
        in_specs=[
            pl.BlockSpec((ROWS_FFN, D_MODEL), lambda i: (i, 0)),
            pl.BlockSpec((1, BATCH, 6 * D_MODEL), lambda i: (layer, 0, 0)),
            pl.BlockSpec((1, 1, D_MODEL), lambda i: (layer, 0, 0)),
            pl.BlockSpec(memory_space=pl.ANY),
            pl.BlockSpec(memory_space=pl.ANY),
            _const_spec((1, D_MODEL)),
        ],
        out_specs=out_spec,
        out_shape=out_shape,
        scratch_shapes=_stage_scratch(w_in.shape) + _stage_scratch(w_out.shape) + store_scratch,
        compiler_params=pltpu.CompilerParams(
            dimension_semantics=("arbitrary",), vmem_limit_bytes=VMEM_LIMIT_BYTES),
        name="ffn_layer",
    )(x2, mods, gain, w_in, w_out, final_g.reshape(1, D_MODEL))


def kernel(x, c, norm1_g, norm2_g, w_ada, b_ada, ssm_a_re, ssm_a_im, ssm_log_step, ssm_b_re, ssm_b_im, ssm_c_re, ssm_c_im, ssm_d, ssm_w_out, conv_w_in, conv_w, conv_w_out, w_ffn_in, w_ffn_out, final_g):
    mods = _ada_mods(c, w_ada, b_ada)
    gain1 = norm1_g.reshape(DEPTH, 1, D_MODEL)
    gain2 = norm2_g.reshape(DEPTH, 1, D_MODEL)
    wz, wy, a2 = _ssm_prep(ssm_a_re, ssm_a_im, ssm_log_step, ssm_b_re, ssm_b_im,
                          ssm_c_re, ssm_c_im)
    ssm_d3 = ssm_d.reshape(-1, 1, D_MODEL)
    x2 = x
    for i in range(DEPTH):
        j = i // N_MIXERS
        if i % N_MIXERS == 0:
            x2 = _s5_layer(x2, mods, i, j, gain1, wz, wy, a2, ssm_d3, ssm_w_out)
        else:
            x2 = _conv_layer(x2, mods, i, j, gain1, conv_w_in, conv_w, conv_w_out)
        last = i == DEPTH - 1
        x2 = _ffn_layer(x2, mods, i, gain2, w_ffn_in, w_ffn_out, final_g,
                        final_norm=last, batch_major_out=last)
    return x2
```

```python
import functools

import jax
import jax.numpy as jnp
from jax import lax
from jax.experimental import pallas as pl
from jax.experimental.pallas import tpu as pltpu

D_MODEL = 1024
BATCH = 8
DEPTH = 4
N_MIXERS = 2
SSM_GROUP = 16
SSM_GROUPS = D_MODEL // SSM_GROUP
SSM_STATE = 64
CONV_WIDTH = 3
RMS_EPS = 1e-6

LANES = 128
SUBLANES = 8
BF16_ROWS = 16
DMA_PRIORITIES = 2
VMEM_LIMIT_BYTES = 56 * 1024 * 1024
STAGE_BYTES = 3 * 512 * 1024

N_CHUNKS = SSM_GROUPS * SSM_STATE // LANES
GROUPS_PER_CHUNK = LANES // SSM_STATE
N_COLS = D_MODEL // LANES
GROUPS_PER_COL = LANES // SSM_GROUP
CHUNKS_PER_COL = N_CHUNKS // N_COLS
COL_K = CHUNKS_PER_COL * 2 * LANES + LANES

ROWS_FFN = 1024
FFN_SUB_BLOCKS = 4
ROWS_CONV = 1024
ROWS_S5 = 512
ADA_COLS = 3072


def _const_spec(shape, index=None):
    index = (0,) * len(shape) if index is None else index
    return pl.BlockSpec(shape, lambda *_: index, pipeline_mode=pl.Buffered(1))


def _layer_spec(shape, layer):
    return _const_spec((1,) + tuple(shape[1:]), (layer,) + (0,) * (len(shape) - 1))


def _norm_modulate(x, gain, shift, scale):
    rows = x.shape[0]
    y = x * lax.rsqrt(jnp.mean(x * x, axis=-1, keepdims=True) + RMS_EPS)
    y = y.reshape(rows // BATCH, BATCH, D_MODEL)
    return (y * (gain * (1.0 + scale))[None] + shift[None]).reshape(rows, D_MODEL)


def _gated_residual(x, gate, update):
    rows = x.shape[0]
    upd = update.reshape(rows // BATCH, BATCH, D_MODEL) * gate[None]
    return x + upd.reshape(rows, D_MODEL)


def _stage_scratch(shape):
    k_dim, n_dim = shape[-2:]
    fits = [r for r in range(BF16_ROWS, k_dim + 1, BF16_ROWS)
            if k_dim % r == 0 and r * n_dim * 4 <= STAGE_BYTES]
    return [pltpu.VMEM((k_dim, n_dim), jnp.bfloat16),
            pltpu.VMEM((2, max(fits), n_dim), jnp.float32),
            pltpu.SemaphoreType.DMA((2,))]


def _stage_bf16(w_hbm, layer, w_bf16, stage, sems):
    chunk = stage.shape[1]
    n_chunks = w_bf16.shape[0] // chunk

    def copy(k):
        return pltpu.make_async_copy(w_hbm.at[layer, pl.ds(k * chunk, chunk), :],
                                     stage.at[k % 2], sems.at[k % 2])

    copy(0).start(priority=0)
    for k in range(n_chunks):
        if k + 1 < n_chunks:
            copy(k + 1).start(priority=(k + 1) % DMA_PRIORITIES)
        copy(k).wait()
        w_bf16[pl.ds(k * chunk, chunk), :] = stage[k % 2].astype(jnp.bfloat16)


def _batch_row_copies(hbm, buf, sems, block, slot, to_hbm):
    steps = buf.shape[1]
    copies = []
    for b in range(BATCH):
        hbm_rows = hbm.at[b, pl.ds(block * steps, steps), :]
        vmem_rows = buf.at[slot, :, b, :]
        src, dst = (vmem_rows, hbm_rows) if to_hbm else (hbm_rows, vmem_rows)
        copies.append(pltpu.make_async_copy(src, dst, sems.at[slot, b]))
    return copies


def _start_all(copies):
    for b, c in enumerate(copies):
        c.start(priority=b % DMA_PRIORITIES)


def _fetch_time_major(x_hbm, buf, sems, n_blocks):
    i = pl.program_id(0)

    @pl.when(i == 0)
    def _():
        _start_all(_batch_row_copies(x_hbm, buf, sems, 0, 0, to_hbm=False))

    @pl.when(i + 1 < n_blocks)
    def _():
        _start_all(_batch_row_copies(x_hbm, buf, sems, i + 1, (i + 1) % 2, to_hbm=False))

    for c in _batch_row_copies(x_hbm, buf, sems, i, i % 2, to_hbm=False):
        c.wait()
    return buf[i % 2].reshape(buf.shape[1] * BATCH, D_MODEL)


def _store_batch_major(value, o_hbm, buf, sems, n_blocks):
    i = pl.program_id(0)
    slot = i % 2

    @pl.when(i >= 2)
    def _():
        for c in _batch_row_copies(o_hbm, buf, sems, i - 2, slot, to_hbm=True):
            c.wait()

    buf[slot] = value.reshape(buf.shape[1], BATCH, D_MODEL)
    _start_all(_batch_row_copies(o_hbm, buf, sems, i, slot, to_hbm=True))

    @pl.when(i == n_blocks - 1)
    def _():
        for back in range(min(2, n_blocks)):
            for c in _batch_row_copies(o_hbm, buf, sems, i - back, (i - back) % 2, to_hbm=True):
                c.wait()


def _ada_kernel(c_ref, w_ref, b_ref, o_ref):
    c_act = jax.nn.silu(c_ref[...])
    o_ref[0] = jnp.dot(c_act, w_ref[0], preferred_element_type=jnp.float32) + b_ref[0]


def _ada_mods(c, w_ada, b_ada):
    n_cols = 6 * D_MODEL
    return pl.pallas_call(
        _ada_kernel,
        grid=(DEPTH, n_cols // ADA_COLS),
        in_specs=[
            pl.BlockSpec((BATCH, D_MODEL), lambda i, j: (0, 0)),
            pl.BlockSpec((1, D_MODEL, ADA_COLS), lambda i, j: (i, 0, j)),
            pl.BlockSpec((1, 1, ADA_COLS), lambda i, j: (i, 0, j)),
        ],
        out_specs=pl.BlockSpec((1, BATCH, ADA_COLS), lambda i, j: (i, 0, j)),
        out_shape=jax.ShapeDtypeStruct((DEPTH, BATCH, n_cols), jnp.float32),
        compiler_params=pltpu.CompilerParams(
            dimension_semantics=("arbitrary", "arbitrary"),
            vmem_limit_bytes=VMEM_LIMIT_BYTES),
        name="ada_mods",
    )(c, w_ada, b_ada.reshape(DEPTH, 1, n_cols))


def _zoh(a_re, a_im, log_step):
    lr = jnp.minimum(a_re, -1e-4)
    li = a_im
    dt = jnp.exp(log_step)
    mag = jnp.exp(lr * dt)
    abr = mag * jnp.cos(li * dt)
    abi = mag * jnp.sin(li * dt)
    den = lr * lr + li * li
    qr = ((abr - 1.0) * lr + abi * li) / den
    qi = (abi * lr - (abr - 1.0) * li) / den
    return abr, abi, qr, qi


def _ssm_param_kernel(al_ref, at_ref, bt_ref, ct_ref, wz_ref, wy_ref, a2_ref):
    bf16 = jnp.bfloat16
    abr, abi, qr, qi = _zoh(al_ref[0, 0], al_ref[1, 0], al_ref[2, 0])
    a2r = abr * abr - abi * abi
    a2i = 2.0 * abr * abi
    a2_ref[0, :, 0] = jnp.broadcast_to(a2r[:, None, :], (N_CHUNKS, SUBLANES, LANES))
    a2_ref[0, :, 1] = jnp.broadcast_to(a2i[:, None, :], (N_CHUNKS, SUBLANES, LANES))
    btr = bt_ref[0, 0]
    bti = bt_ref[1, 0]
    bbr = qr[:, None, :] * btr - qi[:, None, :] * bti
    bbi = qr[:, None, :] * bti + qi[:, None, :] * btr
    bar = abr[:, None, :] * bbr - abi[:, None, :] * bbi
    bai = abr[:, None, :] * bbi + abi[:, None, :] * bbr
    abr_t, abi_t, _, _ = _zoh(at_ref[0, 0], at_ref[1, 0], at_ref[2, 0])
    ctr = ct_ref[0, 0]
    cti = ct_ref[1, 0]

    row_group = lax.broadcasted_iota(jnp.int32, (LANES, LANES), 0) // SSM_GROUP
    col_group = lax.broadcasted_iota(jnp.int32, (LANES, LANES), 1) // SSM_GROUP
    row_sub = lax.broadcasted_iota(jnp.int32, (LANES, LANES), 0) // SSM_STATE
    col_sub = lax.broadcasted_iota(jnp.int32, (LANES, LANES), 1) // SSM_STATE
    lane_sub = lax.broadcasted_iota(jnp.int32, (SSM_GROUP, LANES), 1) // SSM_STATE
    spread = (lax.broadcasted_iota(jnp.int32, (SSM_GROUP, LANES), 1) % SSM_GROUP
              == lax.broadcasted_iota(jnp.int32, (SSM_GROUP, LANES), 0)).astype(bf16)

    def rows_by_group(block):
        rep = jnp.broadcast_to(block[None], (GROUPS_PER_COL, SSM_GROUP, block.shape[-1]))
        return rep.reshape(LANES, block.shape[-1])

    def cols_by_group(blocks):
        flat = blocks.reshape(-1, SSM_GROUP).astype(bf16)
        return jnp.dot(flat, spread, preferred_element_type=jnp.float32)

    for j in range(N_CHUNKS):
        c = j % CHUNKS_PER_COL
        b_live = row_group == c * GROUPS_PER_CHUNK + col_sub
        for half, (re, im) in enumerate(((bar, bai), (bbr, bbi))):
            rows = pl.ds(half * LANES, LANES)
            wz_ref[0, j, rows, pl.ds(0, LANES)] = (
                jnp.where(b_live, rows_by_group(re[j]), 0.0).astype(bf16))
            wz_ref[0, j, rows, pl.ds(LANES, LANES)] = (
                jnp.where(b_live, rows_by_group(im[j]), 0.0).astype(bf16))

    car, cai = [], []
    for j in range(N_CHUNKS):
        a_r, a_i = abr_t[:, j:j + 1], abi_t[:, j:j + 1]
        car.append(ctr[j] * a_r - cti[j] * a_i)
        cai.append(ctr[j] * a_i + cti[j] * a_r)
    car, cai = jnp.stack(car), jnp.stack(cai)
    spread_c = [(cols_by_group(ctr), cols_by_group(car)), (cols_by_group(-cti), cols_by_group(-cai))]
    for j in range(N_CHUNKS):
        v, c = divmod(j, CHUNKS_PER_COL)
        c_live = col_group == c * GROUPS_PER_CHUNK + row_sub
        for part, (even, odd) in enumerate(spread_c):
            rows = pl.ds(c * 2 * LANES + part * LANES, LANES)
            src_rows = slice(j * LANES, (j + 1) * LANES)
            wy_ref[0, v, rows, pl.ds(0, LANES)] = jnp.where(c_live, even[src_rows], 0.0).astype(bf16)
            wy_ref[0, v, rows, pl.ds(LANES, LANES)] = jnp.where(c_live, odd[src_rows], 0.0).astype(bf16)

    cb = []
    for l in range(GROUPS_PER_CHUNK):
        mine = (lane_sub == l)[None]
        cb.append(jnp.einsum('jhk,jko->jho', jnp.where(mine, bbr, 0.0), ctr,
                             preferred_element_type=jnp.float32)
                  - jnp.einsum('jhk,jko->jho', jnp.where(mine, bbi, 0.0), cti,
                               preferred_element_type=jnp.float32))
    cb = cols_by_group(jnp.stack(cb, axis=1))
    on_diag = row_group == col_group
    for v in range(N_COLS):
        rows = pl.ds(CHUNKS_PER_COL * 2 * LANES, LANES)
        wy_ref[0, v, rows, pl.ds(0, LANES)] = jnp.zeros((LANES, LANES), bf16)
        wy_ref[0, v, rows, pl.ds(LANES, LANES)] = (
            jnp.where(on_diag, cb[v * LANES:(v + 1) * LANES], 0.0).astype(bf16))


def _ssm_prep(a_re, a_im, log_step, b_re, b_im, c_re, c_im):
    n = a_re.shape[0]
    f32 = jnp.float32
    p, h = SSM_STATE, SSM_GROUP
    a_lanes = jnp.stack([a_re, a_im, jnp.broadcast_to(log_step[:, :, None], a_re.shape)])
    a_lanes = a_lanes.reshape(3, n, N_CHUNKS, LANES)
    a_rows = jnp.swapaxes(a_lanes, 2, 3)
    b_t = jnp.stack([b_re, b_im]).reshape(2, n, N_CHUNKS, GROUPS_PER_CHUNK, p, h)
    b_t = jnp.transpose(b_t, (0, 1, 2, 5, 3, 4)).reshape(2, n, N_CHUNKS, h, LANES)
    c_t = jnp.stack([c_re, c_im]).reshape(2, n, N_CHUNKS, GROUPS_PER_CHUNK, h, p)
    c_t = jnp.transpose(c_t, (0, 1, 2, 3, 5, 4)).reshape(2, n, N_CHUNKS, LANES, h)
    return pl.pallas_call(
        _ssm_param_kernel,
        grid=(n,),
        in_specs=[
            pl.BlockSpec((3, 1, N_CHUNKS, LANES), lambda i: (0, i, 0, 0)),
            pl.BlockSpec((3, 1, LANES, N_CHUNKS), lambda i: (0, i, 0, 0)),
            pl.BlockSpec((2, 1, N_CHUNKS, h, LANES), lambda i: (0, i, 0, 0, 0)),
            pl.BlockSpec((2, 1, N_CHUNKS, LANES, h), lambda i: (0, i, 0, 0, 0)),
        ],
        out_specs=(
            pl.BlockSpec((1, N_CHUNKS, 2 * LANES, 2 * LANES), lambda i: (i, 0, 0, 0)),
            pl.BlockSpec((1, N_COLS, COL_K, 2 * LANES), lambda i: (i, 0, 0, 0)),
            pl.BlockSpec((1, N_CHUNKS, 2, SUBLANES, LANES), lambda i: (i, 0, 0, 0, 0)),
        ),
        out_shape=(
            jax.ShapeDtypeStruct((n, N_CHUNKS, 2 * LANES, 2 * LANES), jnp.bfloat16),
            jax.ShapeDtypeStruct((n, N_COLS, COL_K, 2 * LANES), jnp.bfloat16),
            jax.ShapeDtypeStruct((n, N_CHUNKS, 2, SUBLANES, LANES), f32),
        ),
        compiler_params=pltpu.CompilerParams(
            dimension_semantics=("arbitrary",), vmem_limit_bytes=VMEM_LIMIT_BYTES),
        name="ssm_params",
    )(a_lanes, a_rows, b_t, c_t)


def _s5_kernel(x_ref, mods_ref, gain_ref, wz_ref, wy_ref, a2_ref, d_ref, wout_hbm,
               o_ref, z_ref, xs_ref, state_ref, oprev_ref, wout_ref, wout_stage, wout_sems,
               *fetch_scratch, pairs, n_blocks, w_layer):
    bf16 = jnp.bfloat16

    @pl.when(pl.program_id(0) == 0)
    def _():
        state_ref[...] = jnp.zeros_like(state_ref)
        oprev_ref[...] = jnp.zeros_like(oprev_ref)
        _stage_bf16(wout_hbm, w_layer, wout_ref, wout_stage, wout_sems)

    if fetch_scratch:
        x = _fetch_time_major(x_ref, *fetch_scratch, n_blocks)
    else:
        x = x_ref[...]
    mods = mods_ref[0]
    shift, scale, gate = (mods[:, k * D_MODEL:(k + 1) * D_MODEL] for k in range(3))

    half_rows = pairs * BATCH
    x_pairs = x.reshape(pairs, 2, BATCH, D_MODEL)
    x_even = x_pairs[:, 0].reshape(half_rows, D_MODEL)
    x_odd = x_pairs[:, 1].reshape(half_rows, D_MODEL)
    h_even = _norm_modulate(x_even, gain_ref[0], shift, scale)
    h_odd = _norm_modulate(x_odd, gain_ref[0], shift, scale)
    h_odd_prev = jnp.concatenate([oprev_ref[...], h_odd[:half_rows - BATCH]], axis=0)
    oprev_ref[...] = h_odd[half_rows - BATCH:]
    hb_even, hb_odd, hb_odd_prev = (t.astype(bf16) for t in (h_even, h_odd, h_odd_prev))

    ys_even, ys_odd = [], []
    for v in range(N_COLS):
        lanes = slice(v * LANES, (v + 1) * LANES)
        chunks = range(v * CHUNKS_PER_COL, (v + 1) * CHUNKS_PER_COL)
        lhs = jnp.concatenate([hb_odd_prev[:, lanes], hb_even[:, lanes]], axis=-1)
        for j in chunks:
            z_ref[j] = jnp.dot(lhs, wz_ref[0, j], preferred_element_type=jnp.float32)

        carry = [(state_ref[j, 0], state_ref[j, 1]) for j in chunks]
        for m0 in range(0, pairs, BF16_ROWS // SUBLANES):
            for c, j in enumerate(chunks):
                sr, si = carry[c]
                a_re, a_im = a2_ref[0, j, 0], a2_ref[0, j, 1]
                outs_r, outs_i = [], []
                for m in range(m0, m0 + BF16_ROWS // SUBLANES):
                    rows = pl.ds(m * SUBLANES, SUBLANES)
                    z_r = z_ref[j, rows, pl.ds(0, LANES)]
                    z_i = z_ref[j, rows, pl.ds(LANES, LANES)]
                    sr, si = (a_re * sr - a_im * si + z_r, a_re * si + a_im * sr + z_i)
                    outs_r.append(sr)
                    outs_i.append(si)
                carry[c] = (sr, si)
                rows = pl.ds(m0 * SUBLANES, BF16_ROWS)
                col = c * 2 * LANES
                xs_ref[v, rows, pl.ds(col, LANES)] = jnp.concatenate(outs_r, axis=0).astype(bf16)
                xs_ref[v, rows, pl.ds(col + LANES, LANES)] = (
                    jnp.concatenate(outs_i, axis=0).astype(bf16))
        for c, j in enumerate(chunks):
            state_ref[j, 0] = carry[c][0]
            state_ref[j, 1] = carry[c][1]

        lhs = jnp.concatenate([xs_ref[v], hb_odd[:, lanes]], axis=-1)
        y = jnp.dot(lhs, wy_ref[0, v], preferred_element_type=jnp.float32)
        ys_even.append(y[:, :LANES])
        ys_odd.append(y[:, LANES:])

    y = jnp.concatenate([jnp.concatenate(ys_even, axis=-1) + d_ref[0] * h_even,
                         jnp.concatenate(ys_odd, axis=-1) + d_ref[0] * h_odd], axis=0)
    y = jax.nn.gelu(y).astype(bf16)
    z = jnp.dot(y, wout_ref[...], preferred_element_type=jnp.float32)
    mix = z[:, :D_MODEL] * jax.nn.sigmoid(z[:, D_MODEL:])
    out = _gated_residual(jnp.concatenate([x_even, x_odd], axis=0), gate, mix)
    out_pairs = jnp.concatenate([out[:half_rows].reshape(pairs, 1, BATCH, D_MODEL),
                                 out[half_rows:].reshape(pairs, 1, BATCH, D_MODEL)], axis=1)
    o_ref[...] = out_pairs.reshape(2 * half_rows, D_MODEL)


def _s5_layer(x_in, mods, layer, j, gain, wz, wy, a2, d, w_out):
    steps = ROWS_S5 // BATCH
    pairs = steps // 2
    batch_major = x_in.ndim == 3
    rows = x_in.shape[0] * x_in.shape[1] if batch_major else x_in.shape[0]
    n_blocks = rows // ROWS_S5
    if batch_major:
        x_spec = pl.BlockSpec(memory_space=pl.ANY)
        fetch_scratch = [pltpu.VMEM((2, steps, BATCH, D_MODEL), jnp.float32),
                         pltpu.SemaphoreType.DMA((2, BATCH))]
    else:
        x_spec = pl.BlockSpec((ROWS_S5, D_MODEL), lambda i: (i, 0))
        fetch_scratch = []
    return pl.pallas_call(
        functools.partial(_s5_kernel, pairs=pairs, n_blocks=n_blocks, w_layer=j),
        grid=(n_blocks,),
        in_specs=[
            x_spec,
            pl.BlockSpec((1, BATCH, 6 * D_MODEL), lambda i: (layer, 0, 0)),
            pl.BlockSpec((1, 1, D_MODEL), lambda i: (layer, 0, 0)),
            _layer_spec(wz.shape, j),
            _layer_spec(wy.shape, j),
            _layer_spec(a2.shape, j),
            _layer_spec(d.shape, j),
            pl.BlockSpec(memory_space=pl.ANY),
        ],
        out_specs=pl.BlockSpec((ROWS_S5, D_MODEL), lambda i: (i, 0)),
        out_shape=jax.ShapeDtypeStruct((rows, D_MODEL), x_in.dtype),
        scratch_shapes=[
            pltpu.VMEM((N_CHUNKS, pairs * BATCH, 2 * LANES), jnp.float32),
            pltpu.VMEM((N_COLS, pairs * BATCH, CHUNKS_PER_COL * 2 * LANES), jnp.bfloat16),
            pltpu.VMEM((N_CHUNKS, 2, SUBLANES, LANES), jnp.float32),
            pltpu.VMEM((BATCH, D_MODEL), jnp.float32),
        ] + _stage_scratch(w_out.shape) + fetch_scratch,
        compiler_params=pltpu.CompilerParams(
            dimension_semantics=("arbitrary",), vmem_limit_bytes=VMEM_LIMIT_BYTES),
        name="s5_layer",
    )(x_in, mods, gain, wz, wy, a2, d, w_out)


def _conv_kernel(x_ref, mods_ref, gain_ref, win_hbm, cw_ref, wout_hbm, o_ref, cv_ref,
                 win_ref, win_stage, win_sems, wout_ref, wout_stage, wout_sems, *, w_layer):
    rows = x_ref.shape[0]
    hist = (CONV_WIDTH - 1) * BATCH

    @pl.when(pl.program_id(0) == 0)
    def _():
        cv_ref[pl.ds(0, hist), :] = jnp.zeros((hist, D_MODEL), jnp.float32)
        _stage_bf16(win_hbm, w_layer, win_ref, win_stage, win_sems)
        _stage_bf16(wout_hbm, w_layer, wout_ref, wout_stage, wout_sems)

    x = x_ref[...]
    mods = mods_ref[0]
    shift, scale, gate = (mods[:, k * D_MODEL:(k + 1) * D_MODEL] for k in range(3))
    h = _norm_modulate(x, gain_ref[0], shift, scale)
    proj = jnp.dot(h.astype(jnp.bfloat16), win_ref[...], preferred_element_type=jnp.float32)
    bg = proj[:, :D_MODEL]
    cv = proj[:, D_MODEL:2 * D_MODEL] * proj[:, 2 * D_MODEL:]
    cv_ref[pl.ds(hist, rows), :] = cv
    cw = cw_ref[0]
    conv = (cw[2:3] * cv + cw[1:2] * cv_ref[pl.ds(BATCH, rows), :]
            + cw[0:1] * cv_ref[pl.ds(0, rows), :])
    cv_ref[pl.ds(0, hist), :] = cv_ref[pl.ds(rows, hist), :]
    out = jnp.dot((bg * conv).astype(jnp.bfloat16), wout_ref[...],
                  preferred_element_type=jnp.float32)
    o_ref[...] = _gated_residual(x, gate, out)


def _conv_layer(x2, mods, layer, j, gain, w_in, conv_w, w_out):
    rows = x2.shape[0]
    hist = (CONV_WIDTH - 1) * BATCH
    return pl.pallas_call(
        functools.partial(_conv_kernel, w_layer=j),
        grid=(rows // ROWS_CONV,),
        in_specs=[
            pl.BlockSpec((ROWS_CONV, D_MODEL), lambda i: (i, 0)),
            pl.BlockSpec((1, BATCH, 6 * D_MODEL), lambda i: (layer, 0, 0)),
            pl.BlockSpec((1, 1, D_MODEL), lambda i: (layer, 0, 0)),
            pl.BlockSpec(memory_space=pl.ANY),
            _layer_spec(conv_w.shape, j),
            pl.BlockSpec(memory_space=pl.ANY),
        ],
        out_specs=pl.BlockSpec((ROWS_CONV, D_MODEL), lambda i: (i, 0)),
        out_shape=jax.ShapeDtypeStruct(x2.shape, x2.dtype),
        scratch_shapes=([pltpu.VMEM((ROWS_CONV + hist, D_MODEL), jnp.float32)]
                        + _stage_scratch(w_in.shape) + _stage_scratch(w_out.shape)),
        compiler_params=pltpu.CompilerParams(
            dimension_semantics=("arbitrary",), vmem_limit_bytes=VMEM_LIMIT_BYTES),
        name="conv_layer",
    )(x2, mods, gain, w_in, conv_w, w_out)


def _ffn_kernel(x_ref, mods_ref, gain_ref, win_hbm, wout_hbm, fg_ref, o_ref,
                win_ref, win_stage, win_sems, wout_ref, wout_stage, wout_sems, *store_scratch,
                final_norm, n_blocks, w_layer):
    hidden = wout_ref.shape[0]

    @pl.when(pl.program_id(0) == 0)
    def _():
        _stage_bf16(win_hbm, w_layer, win_ref, win_stage, win_sems)
        _stage_bf16(wout_hbm, w_layer, wout_ref, wout_stage, wout_sems)

    mods = mods_ref[0]
    shift, scale, gate = (mods[:, k * D_MODEL:(k + 1) * D_MODEL] for k in range(3, 6))
    sub_rows = x_ref.shape[0] // FFN_SUB_BLOCKS
    halves = []
    for q in range(FFN_SUB_BLOCKS):
        rows = pl.ds(q * sub_rows, sub_rows)
        x = x_ref[rows, :]
        h = _norm_modulate(x, gain_ref[0], shift, scale)
        gu = jnp.dot(h.astype(jnp.bfloat16), win_ref[...], preferred_element_type=jnp.float32)
        act = (jax.nn.silu(gu[:, :hidden]) * gu[:, hidden:]).astype(jnp.bfloat16)
        out = jnp.dot(act, wout_ref[...], preferred_element_type=jnp.float32)
        xn = _gated_residual(x, gate, out)
        if final_norm:
            xn = xn * lax.rsqrt(jnp.mean(xn * xn, axis=-1, keepdims=True) + RMS_EPS) * fg_ref[...]
        if store_scratch:
            halves.append(xn)
        else:
            o_ref[rows, :] = xn
    if store_scratch:
        _store_batch_major(jnp.concatenate(halves, axis=0), o_ref, *store_scratch, n_blocks)


def _ffn_layer(x2, mods, layer, gain, w_in, w_out, final_g, final_norm, batch_major_out):
    rows = x2.shape[0]
    n_blocks = rows // ROWS_FFN
    if batch_major_out:
        steps = ROWS_FFN // BATCH
        out_spec = pl.BlockSpec(memory_space=pl.ANY)
        out_shape = jax.ShapeDtypeStruct((BATCH, rows // BATCH, D_MODEL), x2.dtype)
        store_scratch = [pltpu.VMEM((2, steps, BATCH, D_MODEL), jnp.float32),
                         pltpu.SemaphoreType.DMA((2, BATCH))]
    else:
        out_spec = pl.BlockSpec((ROWS_FFN, D_MODEL), lambda i: (i, 0))
        out_shape = jax.ShapeDtypeStruct(x2.shape, x2.dtype)
        store_scratch = []
    return pl.pallas_call(
        functools.partial(_ffn_kernel, final_norm=final_norm, n_blocks=n_blocks, w_layer=layer),
        grid=(n_blocks,),
        in_specs=[
            pl.BlockSpec((ROWS_FFN, D_MODEL), lambda i: (i, 0)),
            pl.BlockSpec((1, BATCH, 6 * D_MODEL), lambda i: (layer, 0, 0)),
            pl.BlockSpec((1, 1, D_MODEL), lambda i: (layer, 0, 0)),
            pl.BlockSpec(memory_space=pl.ANY),
            pl.BlockSpec(memory_space=pl.ANY),
            _const_spec((1, D_MODEL)),
        ],
        out_specs=out_spec,
        out_shape=out_shape,
        scratch_shapes=_stage_scratch(w_in.shape) + _stage_scratch(w_out.shape) + store_scratch,
        compiler_params=pltpu.CompilerParams(
            dimension_semantics=("arbitrary",), vmem_limit_bytes=VMEM_LIMIT_BYTES),
        name="ffn_layer",
    )(x2, mods, gain, w_in, w_out, final_g.reshape(1, D_MODEL))


def kernel(x, c, norm1_g, norm2_g, w_ada, b_ada, ssm_a_re, ssm_a_im, ssm_log_step, ssm_b_re, ssm_b_im, ssm_c_re, ssm_c_im, ssm_d, ssm_w_out, conv_w_in, conv_w, conv_w_out, w_ffn_in, w_ffn_out, final_g):
    mods = _ada_mods(c, w_ada, b_ada)
    gain1 = norm1_g.reshape(DEPTH, 1, D_MODEL)
    gain2 = norm2_g.reshape(DEPTH, 1, D_MODEL)
    wz, wy, a2 = _ssm_prep(ssm_a_re, ssm_a_im, ssm_log_step, ssm_b_re, ssm_b_im,
                          ssm_c_re, ssm_c_im)
    ssm_d3 = ssm_d.reshape(-1, 1, D_MODEL)
    x2 = x
    for i in range(DEPTH):
        j = i // N_MIXERS
        if i % N_MIXERS == 0:
            x2 = _s5_layer(x2, mods, i, j, gain1, wz, wy, a2, ssm_d3, ssm_w_out)
        else:
            x2 = _conv_layer(x2, mods, i, j, gain1, conv_w_in, conv_w, conv_w_out)
        last = i == DEPTH - 1
        x2 = _ffn_layer(x2, mods, i, gain2, w_ffn_in, w_ffn_out, final_g,
                        final_norm=last, batch_major_out=last)
    return x2
```
